```python
import jax, jax.numpy as jnp
from jax import lax
import numpy as np

D_MODEL = 1024
BATCH = 8
SEQ = 2048
DEPTH = 4

A_HEADS = 4
A_HEAD_DIM = 64
A_WIDTH = A_HEADS * A_HEAD_DIM
A_DECAY_LORA = 64
A_ICL_LORA = 64
A_GATE_LORA = 128
A_LN_EPS = 64e-5
B_HEADS = 4
B_HEAD_DIM = 64
B_WIDTH = B_HEADS * B_HEAD_DIM
B_CONV = 4
B_CHUNK = 128
C_Q_HEADS = 8
C_KV_HEADS = 2
C_HEAD_DIM = 64
C_WIDTH = C_Q_HEADS * C_HEAD_DIM
C_KV_WIDTH = C_KV_HEADS * C_HEAD_DIM
WINDOW = 128
C_BLOCK = 128
N_BRANCH = 3
N_GROUPS = 4
EXPERTS_PER_GROUP = 8
N_EXPERTS = N_GROUPS * EXPERTS_PER_GROUP
TOP_K = 2
D_EXPERT = 512
MOE_BLOCK = 128
EPS = 1e-6
ADA_GATE_INIT = 0.5

A_SIZES = (A_WIDTH, A_WIDTH, A_WIDTH, A_DECAY_LORA, A_ICL_LORA, A_GATE_LORA)
A_SHIFT_WIDTH = sum(A_SIZES)
REST_SIZES = (B_WIDTH, B_WIDTH, B_WIDTH, B_WIDTH, B_HEADS, B_HEADS,
              C_WIDTH, C_KV_WIDTH, C_KV_WIDTH, N_BRANCH * D_MODEL)
N_IN = A_SHIFT_WIDTH + sum(REST_SIZES)

kernel_name = 'hybrid_rwkv7_mlstm_swa_hmoe_block'


def _split(z, sizes):
    return jnp.split(z, np.cumsum(sizes)[:-1].tolist(), axis=-1)


def rms_norm(x, g):
    xf = x.astype(jnp.float32)
    y = xf * lax.rsqrt(jnp.mean(xf * xf, axis=-1, keepdims=True) + EPS)
    return (y * g).astype(x.dtype)


def token_shift(z):
    return jnp.pad(z, ((0, 0), (1, 0), (0, 0)))[:, :-1]


def causal_conv(x, w, b):
    K, C = w.shape
    y = lax.conv_general_dilated(x, w[:, None, :], window_strides=(1,), padding=((K - 1, 0),),
                                 dimension_numbers=('NWC', 'WIO', 'NWC'), feature_group_count=C)
    return y + b


def alibi_slopes(n):
    return jnp.exp2(-8.0 * jnp.arange(1, n + 1, dtype=jnp.float32) / n)


def rwkv7_time_mix(r, k, v, w_lo, a_lo, g_lo, w0, w2, a0, a2, g2, k_k, k_a, r_k, lnx_g, lnx_b):
    dt = r.dtype
    f32 = jnp.float32
    Bsz, T, _ = r.shape
    H, N = A_HEADS, A_HEAD_DIM
    r, k, v, w_lo, a_lo, g_lo = (z.astype(f32) for z in (r, k, v, w_lo, a_lo, g_lo))
    w_log = -jax.nn.softplus(-(w0 + jnp.tanh(w_lo) @ w2)) - 0.5
    decay = jnp.exp(-jnp.exp(w_log))
    a = jax.nn.sigmoid(a0 + a_lo @ a2)
    g = jax.nn.sigmoid(g_lo) @ g2
    heads = lambda z: z.reshape(Bsz, T, H, N)
    kk = heads(k * k_k)
    kk = kk / jnp.maximum(jnp.sqrt(jnp.sum(kk * kk, axis=-1, keepdims=True)), 1e-12)
    k = k * (1 + (a - 1) * k_a)
    rh, kh, vh, ah, wh = heads(r), heads(k), heads(v), heads(a), heads(decay)
    a_vec = -kk
    b_vec = kk * ah

    def step(S, inp):
        r_t, w_t, k_t, v_t, av_t, bv_t = inp
        sa = jnp.einsum('bhij,bhj->bhi', S, av_t)
        S = S * w_t[:, :, None, :] + sa[..., None] * bv_t[:, :, None, :] + v_t[..., None] * k_t[:, :, None, :]
        return S, jnp.einsum('bhij,bhj->bhi', S, r_t)

    S0 = jnp.zeros((Bsz, H, N, N), f32)
    _, y = lax.scan(step, S0, tuple(jnp.moveaxis(z, 1, 0) for z in (rh, wh, kh, vh, a_vec, b_vec)))
    y = jnp.moveaxis(y, 0, 1)
    mu = jnp.mean(y, axis=-1, keepdims=True)
    var = jnp.mean(jnp.square(y - mu), axis=-1, keepdims=True)
    y = ((y - mu) * lax.rsqrt(var + A_LN_EPS)).reshape(Bsz, T, A_WIDTH) * lnx_g + lnx_b
    bonus = jnp.sum(rh * kh * r_k, axis=-1, keepdims=True) * vh
    y = (y + bonus.reshape(Bsz, T, A_WIDTH)) * g
    return y.astype(dt)


def mlstm_chunkwise(q, k, v, o, i_pre, f_pre, conv_w, conv_b, b_i, b_f, hn_g):
    dt = q.dtype
    f32 = jnp.float32
    Bsz, T, _ = q.shape
    H, d, L = B_HEADS, B_HEAD_DIM, B_CHUNK
    NC = T // L
    qk = jax.nn.silu(causal_conv(jnp.concatenate([q, k], axis=-1), conv_w, conv_b))
    q, k = jnp.split(qk, 2, axis=-1)
    chunks = lambda z: z.reshape(Bsz, NC, L, H, d).transpose(0, 3, 1, 2, 4).astype(f32)
    qc, kc, vc = chunks(q), chunks(k) * d ** -0.5, chunks(v)
    gate_chunks = lambda z: z.astype(f32).reshape(Bsz, NC, L, H).transpose(0, 3, 1, 2)
    ig = gate_chunks(i_pre + b_i)
    logf = jax.nn.log_sigmoid(gate_chunks(f_pre + b_f))
    F = jnp.cumsum(logf, axis=-1)
    F_last = F[..., -1]
    g_end = F_last[..., None] - F + ig

    def chunk_step(carry, inp):
        C, n, m = carry
        k_c, v_c, g_c, fl_c = inp
        m_new = jnp.maximum(fl_c + m, jnp.max(g_c, axis=-1))
        carry_decay = jnp.exp(fl_c + m - m_new)
        w = jnp.exp(g_c - m_new[..., None])
        C_new = carry_decay[..., None, None] * C + jnp.einsum('bhs,bhsd,bhse->bhde', w, v_c, k_c)
        n_new = carry_decay[..., None] * n + jnp.einsum('bhs,bhse->bhe', w, k_c)
        return (C_new, n_new, m_new), (C, n, m)

    init = (jnp.zeros((Bsz, H, d, d), f32), jnp.zeros((Bsz, H, d), f32), jnp.zeros((Bsz, H), f32))
    cm = lambda z: jnp.moveaxis(z, 2, 0)
    _, (C0, n0, m0) = lax.scan(chunk_step, init, (cm(kc), cm(vc), cm(g_end), cm(F_last)))
    C0 = jnp.moveaxis(C0, 0, 2)
    n0 = jnp.moveaxis(n0, 0, 2)
    m0 = jnp.moveaxis(m0, 0, 2)
    D = F[..., :, None] - F[..., None, :] + ig[..., None, :]
    D = jnp.where(jnp.tril(jnp.ones((L, L), bool)), D, -jnp.inf)
    g_inter = F + m0[..., None]
    m_t = jnp.maximum(g_inter, jnp.max(D, axis=-1))
    Sw = jnp.einsum('bhctd,bhcsd->bhcts', qc, kc) * jnp.exp(D - m_t[..., None])
    inter = jnp.exp(g_inter - m_t)
    num = jnp.einsum('bhcts,bhcsd->bhctd', Sw, vc) + inter[..., None] * jnp.einsum('bhcde,bhcte->bhctd', C0, qc)
    den = jnp.sum(Sw, axis=-1) + inter * jnp.einsum('bhcd,bhctd->bhct', n0, qc)
    h = num / jnp.maximum(jnp.abs(den), jnp.exp(-m_t))[..., None]
    h = h.transpose(0, 2, 3, 1, 4).reshape(Bsz, T, H, d)
    h = h * lax.rsqrt(jnp.mean(h * h, axis=-1, keepdims=True) + EPS) * hn_g
    h = h.reshape(Bsz, T, B_WIDTH) * jax.nn.sigmoid(o.astype(f32))
    return h.astype(dt)


def sliding_window_sink_attention(q, k, v, qn_g, kn_g, sinks):
    dt = q.dtype
    f32 = jnp.float32
    Bsz, T, _ = q.shape
    Hkv, d, Bk = C_KV_HEADS, C_HEAD_DIM, C_BLOCK
    G = C_Q_HEADS // Hkv
    NB = T // Bk

    def hnorm(z, g):
        z = z.astype(f32)
        return z * lax.rsqrt(jnp.mean(z * z, axis=-1, keepdims=True) + EPS) * g

    qh = hnorm(q.reshape(Bsz, NB, Bk, Hkv, G, d), qn_g) * d ** -0.5
    kh = hnorm(k.reshape(Bsz, NB, Bk, Hkv, d), kn_g)
    vh = v.reshape(Bsz, NB, Bk, Hkv, d).astype(f32)

    def band(z):
        prev = jnp.pad(z, ((0, 0), (1, 0), (0, 0), (0, 0), (0, 0)))[:, :-1]
        return jnp.concatenate([prev, z], axis=2)

    kb, vb = band(kh), band(vh)
    s = jnp.einsum('bnqkgd,bnskd->bkgnqs', qh, kb)
    blk = jnp.arange(NB)[:, None, None]
    q_pos = blk * Bk + jnp.arange(Bk)[None, :, None]
    k_pos = (blk - 1) * Bk + jnp.arange(2 * Bk)[None, None, :]
    dist = q_pos - k_pos
    valid = (dist >= 0) & (dist < WINDOW) & (k_pos >= 0)
    slopes = alibi_slopes(C_Q_HEADS).reshape(Hkv, G)[None, :, :, None, None, None]
    s = jnp.where(valid, s - slopes * dist.astype(f32), -jnp.inf)
    sink = sinks.astype(f32).reshape(Hkv, G)[None, :, :, None, None, None]
    m = jnp.maximum(jnp.max(s, axis=-1, keepdims=True), sink)
    p = jnp.exp(s - m)
    den = jnp.sum(p, axis=-1, keepdims=True) + jnp.exp(sink - m)
    out = jnp.einsum('bkgnqs,bnskd->bnqkgd', p / den, vb)
    return out.reshape(Bsz, T, C_WIDTH).astype(dt)


def hybrid_mixer(h, w_in, mu_a, w0_a, w2_a, a0_a, a2_a, g2_a, kk_a, ka_a, rk_a, lnx_g_a, lnx_b_a,
                 conv_w_b, conv_b_b, bi_b, bf_b, hn_g_b, qn_g_c, kn_g_c, sink_c,
                 w_br_a, w_br_b, w_br_c, w_out):
    p = h @ w_in
    p_a, p_rest = p[..., :A_SHIFT_WIDTH], p[..., A_SHIFT_WIDTH:]
    p_a = p_a + mu_a * (token_shift(p_a) - p_a)
    rA, kA, vA, wloA, aloA, gloA = _split(p_a, A_SIZES)
    qB, kB, vB, oB, iB, fB, qC, kC, vC, gate_logits = _split(p_rest, REST_SIZES)
    y_a = rwkv7_time_mix(rA, kA, vA, wloA, aloA, gloA, w0_a, w2_a, a0_a, a2_a, g2_a,
                         kk_a, ka_a, rk_a, lnx_g_a, lnx_b_a)
    y_b = mlstm_chunkwise(qB, kB, vB, oB, iB, fB, conv_w_b, conv_b_b, bi_b, bf_b, hn_g_b)
    y_c = sliding_window_sink_attention(qC, kC, vC, qn_g_c, kn_g_c, sink_c)
    g_a, g_b, g_c = jnp.split(jax.nn.sigmoid(gate_logits), N_BRANCH, axis=-1)
    merged = g_a * (y_a @ w_br_a) + g_b * (y_b @ w_br_b) + g_c * (y_c @ w_br_c)
    return merged @ w_out


def hier_moe(h, w_rg, b_rg, w_re, b_re, w_e_in, w_e_out):
    f32 = jnp.float32
    Bsz, T, D = h.shape
    x = h.reshape(-1, D)
    NT = x.shape[0]
    g_prob = jax.nn.softmax((x @ w_rg).astype(f32) + b_rg.astype(f32), axis=-1)
    g_top, g_idx = lax.top_k(g_prob, 1)
    e_logits = ((x @ w_re).astype(f32) + b_re.astype(f32)).reshape(NT, N_GROUPS, EXPERTS_PER_GROUP)
    e_in_group = jnp.take_along_axis(e_logits, g_idx[:, :, None], axis=1)[:, 0]
    e_top, e_idx = lax.top_k(jax.nn.softmax(e_in_group, axis=-1), TOP_K)
    weights = g_top * e_top / jnp.sum(e_top, axis=-1, keepdims=True)
    expert = g_idx * EXPERTS_PER_GROUP + e_idx
    NA = NT * TOP_K
    a_exp = expert.reshape(-1).astype(jnp.int32)
    a_tok = jnp.repeat(jnp.arange(NT, dtype=jnp.int32), TOP_K)
    a_w = weights.reshape(-1)
    order = jnp.argsort(a_exp)
    s_exp = a_exp[order]
    counts = jax.ops.segment_sum(jnp.ones((NA,), jnp.int32), a_exp, num_segments=N_EXPERTS)
    starts = jnp.cumsum(counts) - counts
    padded = (counts + MOE_BLOCK - 1) // MOE_BLOCK * MOE_BLOCK
    pad_ends = jnp.cumsum(padded)
    pad_starts = pad_ends - padded
    dest = pad_starts[s_exp] + jnp.arange(NA, dtype=jnp.int32) - starts[s_exp]
    n_blocks = -(-NA // MOE_BLOCK) + N_EXPERTS
    n_slots = n_blocks * MOE_BLOCK
    slot_tok = jnp.zeros((n_slots,), jnp.int32).at[dest].set(a_tok[order])
    slot_w = jnp.zeros((n_slots,), f32).at[dest].set(a_w[order])
    block_exp = jnp.minimum(jnp.searchsorted(pad_ends, jnp.arange(n_blocks, dtype=jnp.int32) * MOE_BLOCK,
                                             side='right'), N_EXPERTS - 1)
    xb = x[slot_tok].reshape(n_blocks, MOE_BLOCK, D)

    def expert_block(args):
        xs, e = args
        gate, up = jnp.split(xs @ w_e_in[e], 2, axis=-1)
        return (jax.nn.silu(gate) * up) @ w_e_out[e]

    yb = lax.map(expert_block, (xb, block_exp)).reshape(n_slots, D)
    y = jnp.zeros((NT, D), f32).at[slot_tok].add(yb.astype(f32) * slot_w[:, None])
    return y.reshape(Bsz, T, D).astype(h.dtype)


def setup_inputs(seed: int = 0) -> dict:
    key = jax.random.key(seed)
    k = jax.random.split(key, 36)
    L, D = DEPTH, D_MODEL
    f32 = jnp.float32
    nrm = lambda kk, shape, std: std * jax.random.normal(kk, shape, f32)
    uni = lambda kk, shape, lo, hi: jax.random.uniform(kk, shape, f32, minval=lo, maxval=hi)
    ada_offset = jnp.array([0.0, 0.0, ADA_GATE_INIT, 0.0, 0.0, ADA_GATE_INIT], f32)[None, :, None]
    return {
        'x': nrm(k[0], (BATCH, SEQ, D), 1.0),
        'c': nrm(k[1], (BATCH, D), 1.0),
        'w_ada': nrm(k[2], (L, D, 6 * D), 0.1 * D ** -0.5),
        'b_ada': (nrm(k[3], (L, 6, D), 0.02) + ada_offset).reshape(L, 6 * D),
        'norm1_g': 1.0 + nrm(k[4], (L, D), 0.02),
        'norm2_g': 1.0 + nrm(k[5], (L, D), 0.02),
        'w_in': nrm(k[6], (L, D, N_IN), D ** -0.5),
        'mu_a': uni(k[7], (L, A_SHIFT_WIDTH), 0.0, 1.0),
        'w0_a': uni(k[8], (L, A_WIDTH), -6.0, 1.0),
        'w2_a': nrm(k[9], (L, A_DECAY_LORA, A_WIDTH), 0.1),
        'a0_a': nrm(k[10], (L, A_WIDTH), 0.5),
        'a2_a': nrm(k[11], (L, A_ICL_LORA, A_WIDTH), 0.5 * A_ICL_LORA ** -0.5),
        'g2_a': nrm(k[12], (L, A_GATE_LORA, A_WIDTH), A_GATE_LORA ** -0.5),
        'kk_a': 0.85 + nrm(k[13], (L, A_WIDTH), 0.1),
        'ka_a': 1.0 + nrm(k[14], (L, A_WIDTH), 0.1),
        'rk_a': nrm(k[15], (L, A_HEADS, A_HEAD_DIM), 0.1),
        'lnx_g_a': 1.0 + nrm(k[16], (L, A_WIDTH), 0.02),
        'lnx_b_a': nrm(k[17], (L, A_WIDTH), 0.02),
        'conv_w_b': nrm(k[18], (L, B_CONV, 2 * B_WIDTH), B_CONV ** -0.5),
        'conv_b_b': nrm(k[19], (L, 2 * B_WIDTH), 0.02),
        'bi_b': nrm(k[20], (L, B_HEADS), 0.1),
        'bf_b': jnp.linspace(3.0, 6.0, B_HEADS, dtype=f32)[None] + nrm(k[21], (L, B_HEADS), 0.1),
        'hn_g_b': 1.0 + nrm(k[22], (L, B_HEADS, B_HEAD_DIM), 0.02),
        'qn_g_c': 1.0 + nrm(k[23], (L, C_HEAD_DIM), 0.02),
        'kn_g_c': 1.0 + nrm(k[24], (L, C_HEAD_DIM), 0.02),
        'sink_c': nrm(k[25], (L, C_Q_HEADS), 0.5),
        'w_br_a': nrm(k[26], (L, A_WIDTH, D), A_WIDTH ** -0.5),
        'w_br_b': nrm(k[27], (L, B_WIDTH, D), B_WIDTH ** -0.5),
        'w_br_c': nrm(k[28], (L, C_WIDTH, D), C_WIDTH ** -0.5),
        'w_out': nrm(k[29], (L, D, D), D ** -0.5),
        'w_rg': nrm(k[30], (L, D, N_GROUPS), D ** -0.5),
        'b_rg': nrm(k[31], (L, N_GROUPS), 0.01),
        'w_re': nrm(k[32], (L, D, N_EXPERTS), D ** -0.5),
        'b_re': nrm(k[33], (L, N_EXPERTS), 0.01),
        'w_e_in': nrm(k[34], (L, N_EXPERTS, D, 2 * D_EXPERT), D ** -0.5),
        'w_e_out': nrm(k[35], (L, N_EXPERTS, D_EXPERT, D), D_EXPERT ** -0.5),
    }


def reference(x, c, w_ada, b_ada, norm1_g, norm2_g, w_in, mu_a, w0_a, w2_a, a0_a, a2_a, g2_a,
              kk_a, ka_a, rk_a, lnx_g_a, lnx_b_a, conv_w_b, conv_b_b, bi_b, bf_b, hn_g_b,
              qn_g_c, kn_g_c, sink_c, w_br_a, w_br_b, w_br_c, w_out, w_rg, b_rg, w_re, b_re,
              w_e_in, w_e_out):
    cond = jax.nn.silu(c)
    for l in range(DEPTH):
        mod = cond @ w_ada[l] + b_ada[l]
        sh1, sc1, gt1, sh2, sc2, gt2 = jnp.split(mod[:, None, :], 6, axis=-1)
        h = rms_norm(x, norm1_g[l]) * (1 + sc1) + sh1
        y = hybrid_mixer(h, w_in[l], mu_a[l], w0_a[l], w2_a[l], a0_a[l], a2_a[l], g2_a[l],
                         kk_a[l], ka_a[l], rk_a[l], lnx_g_a[l], lnx_b_a[l],
                         conv_w_b[l], conv_b_b[l], bi_b[l], bf_b[l], hn_g_b[l],
                         qn_g_c[l], kn_g_c[l], sink_c[l],
                         w_br_a[l], w_br_b[l], w_br_c[l], w_out[l])
        x = x + gt1 * y
        h = rms_norm(x, norm2_g[l]) * (1 + sc2) + sh2
        x = x + gt2 * hier_moe(h, w_rg[l], b_rg[l], w_re[l], b_re[l], w_e_in[l], w_e_out[l])
    return x
```

```python
import functools

import numpy as np
import jax
import jax.numpy as jnp
from jax import lax
from jax.experimental import pallas as pl
from jax.experimental.pallas import tpu as pltpu

F32 = jnp.float32
BF16 = jnp.bfloat16
HIGHEST = lax.Precision.HIGHEST

A_HEADS = 4
A_HEAD_DIM = 64
A_WIDTH = 256
A_LN_EPS = 64e-5
B_HEADS = 4
B_HEAD_DIM = 64
B_WIDTH = 256
B_CHUNK = 128
C_Q_HEADS = 8
C_KV_HEADS = 2
C_HEAD_DIM = 64
C_WIDTH = 512
C_KV_WIDTH = 128
WINDOW = 128
C_BLOCK = 128
N_GROUPS = 4
EXPERTS_PER_GROUP = 8
N_EXPERTS = 32
TOP_K = 2
D_EXPERT = 512
MOE_BLOCK = 128
EPS = 1e-6

LANES = 128
SUBLANES = 8
VMEM_LIMIT = 56 * 1024 * 1024

ROW_TILE = 256
SCAN_STEPS = 16


def _mm(a, b):
    return jnp.dot(a.astype(BF16), b.astype(BF16), preferred_element_type=F32)


def _mm32(a, b):
    return jnp.dot(a, b, precision=HIGHEST, preferred_element_type=F32)


def _mm_nt(a, b):
    return lax.dot_general(a.astype(BF16), b.astype(BF16), (((1,), (1,)), ((), ())),
                           preferred_element_type=F32)


def _mm_tn(a, b):
    return lax.dot_general(a.astype(BF16), b.astype(BF16), (((0,), (0,)), ((), ())),
                           preferred_element_type=F32)


def _log_sigmoid(x):
    return jnp.minimum(x, 0.0) - jnp.log(1.0 + jnp.exp(-jnp.abs(x)))


def _params(n_axes=1):
    return pltpu.CompilerParams(dimension_semantics=("arbitrary",) * n_axes,
                                vmem_limit_bytes=VMEM_LIMIT)


def _full(shape):
    nd = len(shape)
    return pl.BlockSpec(shape, lambda *_: (0,) * nd)


def _ada_kernel(c_ref, w_ref, b_ref, o_ref):
    c = c_ref[...]
    cond = c * jax.nn.sigmoid(c)
    o_ref[0] = _mm(cond, w_ref[0]) + b_ref[0]


def _ada_mod(c, w_ada, b_ada):
    depth, d, n = w_ada.shape
    bsz = c.shape[0]
    tn = 1536
    return pl.pallas_call(
        _ada_kernel,
        grid=(depth, n // tn),
        in_specs=[pl.BlockSpec((bsz, d), lambda l, j: (0, 0)),
                  pl.BlockSpec((1, d, tn), lambda l, j: (l, 0, j)),
                  pl.BlockSpec((1, 1, tn), lambda l, j: (l, 0, j))],
        out_specs=pl.BlockSpec((1, bsz, tn), lambda l, j: (l, 0, j)),
        out_shape=jax.ShapeDtypeStruct((depth, bsz, n), F32),
        compiler_params=_params(2),
        name="ada_mod",
    )(c, w_ada, b_ada.reshape(depth, 1, n))


def _norm_proj_kernel(x_ref, g_ref, sc_ref, sh_ref, *refs):
    n = len(refs) // 2
    x = x_ref[...]
    y = x * lax.rsqrt(jnp.mean(x * x, axis=-1, keepdims=True) + EPS) * g_ref[...]
    h = (y * (1.0 + sc_ref[0]) + sh_ref[0]).astype(BF16)
    for w_ref, o_ref in zip(refs[:n], refs[n:]):
        o_ref[...] = jnp.dot(h, w_ref[...], preferred_element_type=F32)


def _norm_proj(x2, g, sc, sh, weights, seq):
    nt, d = x2.shape
    tm = ROW_TILE
    tps = seq // tm
    row = lambda w: pl.BlockSpec((tm, w), lambda i: (i, 0))
    mod = pl.BlockSpec((1, 1, d), lambda i: (i // tps, 0, 0))
    return pl.pallas_call(
        _norm_proj_kernel,
        grid=(nt // tm,),
        in_specs=[row(d), _full((1, d)), mod, mod] + [_full(w.shape) for w in weights],
        out_specs=[row(w.shape[1]) for w in weights],
        out_shape=[jax.ShapeDtypeStruct((nt, w.shape[1]), F32) for w in weights],
        compiler_params=_params(),
        name="norm_proj",
    )(x2, g, sc, sh, *weights)


def _rwkv_prep_kernel(p_ref, prev_ref, mu_ref, wwa_ref, w0_ref, a0_ref, g2_ref, kkw_ref, kaw_ref, hsum_ref,
                      r_out, w_out, k_out, v_out, kk_out, b_out, g_out, *, tiles_per_seq):
    i = pl.program_id(0)
    p = p_ref[...]
    first = (i % tiles_per_seq) == 0
    prev_row = jnp.where(first, 0.0, prev_ref[SUBLANES - 1:SUBLANES, :])
    row = lax.broadcasted_iota(jnp.int32, p.shape, 0)
    shifted = jnp.where(row == 0, prev_row, pltpu.roll(p, 1, 0))
    pa = p + mu_ref[...] * (shifted - p)
    r = pa[:, 0:256]
    k = pa[:, 256:512]
    v = pa[:, 512:768]
    slab = pa[:, 768:896]
    glo = pa[:, 896:1024]
    lane = lax.broadcasted_iota(jnp.int32, slab.shape, 1)
    z = jnp.where(lane < 64, jnp.tanh(slab), slab)
    wa = _mm32(z, wwa_ref[...])
    w_log = _log_sigmoid(w0_ref[...] + wa[:, :256]) - 0.5
    decay = jnp.exp(-jnp.exp(w_log))
    a = jax.nn.sigmoid(a0_ref[...] + wa[:, 256:])
    g = _mm32(jax.nn.sigmoid(glo), g2_ref[...])
    kk = k * kkw_ref[...]
    ss = _mm32(kk * kk, hsum_ref[...])
    kk = kk / jnp.maximum(jnp.sqrt(ss), 1e-12)
    r_out[...] = r
    w_out[...] = decay
    k_out[...] = k * (1.0 + (a - 1.0) * kaw_ref[...])
    v_out[...] = v
    kk_out[...] = kk
    b_out[...] = kk * a
    g_out[...] = g


def _rwkv_prep(p_a, mu, wwa, w0, a0, g2, kkw, kaw, hsum, seq):
    nt, wd = p_a.shape
    tm = ROW_TILE
    tps = seq // tm
    row = lambda w: pl.BlockSpec((tm, w), lambda i: (i, 0))
    prev = pl.BlockSpec((SUBLANES, wd), lambda i: (jnp.maximum(i * (tm // SUBLANES) - 1, 0), 0))
    outs = [jax.ShapeDtypeStruct((nt, A_WIDTH), F32)] * 7
    return pl.pallas_call(
        functools.partial(_rwkv_prep_kernel, tiles_per_seq=tps),
        grid=(nt // tm,),
        in_specs=[row(wd), prev, _full(mu.shape), _full(wwa.shape), _full(w0.shape), _full(a0.shape),
                  _full(g2.shape), _full(kkw.shape), _full(kaw.shape), _full(hsum.shape)],
        out_specs=[row(A_WIDTH)] * 7,
        out_shape=outs,
        compiler_params=_params(),
        name="rwkv_prep",
    )(p_a, p_a, mu, wwa, w0, a0, g2, kkw, kaw, hsum)


def _rwkv_scan_kernel(w_ref, kk_ref, b_ref, k_ref, r_ref, v_ref, y_ref, s_ref, *, steps):
    @pl.when(pl.program_id(0) == 0)
    def _():
        s_ref[...] = jnp.zeros_like(s_ref)

    def tree(terms):
        while len(terms) > 1:
            terms = [terms[i] + terms[i + 1] for i in range(0, len(terms), 2)]
        return terms[0]

    def step(t, carry):
        for ih in range(2):
            base = ih * A_HEAD_DIM
            v_t = v_ref[t, ih * SUBLANES:(ih + 1) * SUBLANES, :]
            sa = tree([s_ref[base + j] * kk_ref[t, j:j + 1, :] for j in range(A_HEAD_DIM)])
            ys = []
            for j in range(A_HEAD_DIM):
                s = (s_ref[base + j] * w_ref[t, j:j + 1, :] - sa * b_ref[t, j:j + 1, :]
                     + v_t * k_ref[t, j:j + 1, :])
                s_ref[base + j] = s
                ys.append(s * r_ref[t, j:j + 1, :])
            y_ref[t, ih * SUBLANES:(ih + 1) * SUBLANES, :] = tree(ys)
        return carry

    lax.fori_loop(0, steps, step, 0)


def _rwkv_scan(w_e, kk_e, b_e, k_e, r_e, v_p):
    seq = w_e.shape[0]
    tc = SCAN_STEPS
    key = pl.BlockSpec((tc, A_HEAD_DIM, LANES), lambda i: (i, 0, 0))
    val = pl.BlockSpec((tc, 2 * SUBLANES, LANES), lambda i: (i, 0, 0))
    return pl.pallas_call(
        functools.partial(_rwkv_scan_kernel, steps=tc),
        grid=(seq // tc,),
        in_specs=[key] * 5 + [val],
        out_specs=val,
        out_shape=jax.ShapeDtypeStruct((seq, 2 * SUBLANES, LANES), F32),
        scratch_shapes=[pltpu.VMEM((2 * A_HEAD_DIM, SUBLANES, LANES), F32)],
        compiler_params=_params(),
        name="rwkv_scan",
    )(w_e, kk_e, b_e, k_e, r_e, v_p)


def _expand_key(x, bsz, seq):
    x = x.reshape(bsz, seq, A_HEADS, A_HEAD_DIM).transpose(1, 3, 0, 2)
    x = jnp.broadcast_to(x[..., None], (seq, A_HEAD_DIM, bsz, A_HEADS, 4))
    return x.reshape(seq, A_HEAD_DIM, LANES)


def _permute_value(x, bsz, seq):
    x = x.reshape(bsz, seq, A_HEADS, 2, SUBLANES, 4).transpose(1, 3, 4, 0, 2, 5)
    return x.reshape(seq, 2 * SUBLANES, LANES)


def _unpermute_value(y, bsz, seq):
    y = y.reshape(seq, 2, SUBLANES, bsz, A_HEADS, 4).transpose(3, 0, 4, 1, 2, 5)
    return y.reshape(bsz * seq, A_WIDTH)


def _mlstm_kernel(pb_ref, prev_ref, pif_ref, gt_ref, cw_ref, cb_ref, bifr_ref, bifc_ref, hng_ref,
                  ltri_ref, utri_ref, o_ref, c_ref, n_ref, m_ref, *, chunks_per_seq):
    i = pl.program_id(0)
    first = (i % chunks_per_seq) == 0

    @pl.when(first)
    def _():
        c_ref[...] = jnp.zeros_like(c_ref)
        n_ref[...] = jnp.zeros_like(n_ref)
        m_ref[...] = jnp.zeros_like(m_ref)

    L = B_CHUNK
    pb = pb_ref[...]
    x = pb[:, :2 * B_WIDTH]
    prev = jnp.where(first, 0.0, prev_ref[:, :2 * B_WIDTH])
    cw = cw_ref[...]
    taps = cw.shape[0]
    acc = x * cw[taps - 1:taps, :] + cb_ref[...]
    r8 = lax.broadcasted_iota(jnp.int32, prev.shape, 0)
    for s in range(1, taps):
        xs = pltpu.roll(x, s, 0)
        head = jnp.where(r8 < s, pltpu.roll(prev, s, 0), xs[:SUBLANES])
        xs = jnp.concatenate([head, xs[SUBLANES:]], axis=0)
        acc = acc + xs * cw[taps - 1 - s:taps - s, :]
    qk = acc * jax.nn.sigmoid(acc)
    q = qk[:, :B_WIDTH]
    k = qk[:, B_WIDTH:] * (B_HEAD_DIM ** -0.5)
    v = pb[:, 2 * B_WIDTH:3 * B_WIDTH]
    o = pb[:, 3 * B_WIDTH:]

    pif = pif_ref[...] + bifr_ref[...]
    lane = lax.broadcasted_iota(jnp.int32, pif.shape, 1)
    logf_c = jnp.where(lane >= B_HEADS, _log_sigmoid(pif), 0.0)
    f_c = _mm32(ltri_ref[...], logf_c)
    g = gt_ref[0] + bifc_ref[...]
    row = lax.broadcasted_iota(jnp.int32, g.shape, 0)
    logf_r = jnp.where(row >= B_HEADS, _log_sigmoid(g), 0.0)
    f_r = _mm32(logf_r, utri_ref[...])

    ti = lax.broadcasted_iota(jnp.int32, (L, L), 0)
    si = lax.broadcasted_iota(jnp.int32, (L, L), 1)
    causal = si <= ti
    hng = hng_ref[...]

    for h in range(B_HEADS):
        sl = slice(h * B_HEAD_DIM, (h + 1) * B_HEAD_DIM)
        fc = f_c[:, B_HEADS + h:B_HEADS + h + 1]
        igc = pif[:, h:h + 1]
        fr = f_r[B_HEADS + h:B_HEADS + h + 1, :]
        igr = g[h:h + 1, :]
        f_last = fr[:, L - 1:L]
        m0 = m_ref[h]
        c0 = c_ref[h]
        n0 = n_ref[h]
        qh, kh, vh = q[:, sl], k[:, sl], v[:, sl]

        d = jnp.where(causal, fc - fr + igr, -jnp.inf)
        g_inter = fc + m0
        m_t = jnp.maximum(g_inter, jnp.max(d, axis=-1, keepdims=True))
        sw = _mm_nt(qh, kh) * jnp.exp(d - m_t)
        inter = jnp.exp(g_inter - m_t)
        num = _mm(sw, vh) + inter * _mm_nt(qh, c0)
        den = jnp.sum(sw, axis=-1, keepdims=True) + inter * jnp.sum(qh * n0, axis=-1, keepdims=True)
        hh = num / jnp.maximum(jnp.abs(den), jnp.exp(-m_t))
        hh = hh * lax.rsqrt(jnp.mean(hh * hh, axis=-1, keepdims=True) + EPS) * hng[:, sl]
        o_ref[:, sl] = hh * jax.nn.sigmoid(o[:, sl])

        g_end_r = f_last - fr + igr
        m_new = jnp.maximum(f_last + m0, jnp.max(g_end_r, axis=-1, keepdims=True))
        carry_decay = jnp.exp(f_last + m0 - m_new)
        w_c = jnp.exp(f_last - fc + igc - m_new)
        c_ref[h] = carry_decay * c0 + _mm_tn(vh * w_c, kh)
        n_ref[h] = carry_decay * n0 + jnp.sum(kh * w_c, axis=0, keepdims=True)
        m_ref[h] = m_new


def _mlstm(p_b, p_if, g_t, conv_w, conv_b, bif_row, bif_col, hn_g, ltri, utri, seq):
    nt = p_b.shape[0]
    L = B_CHUNK
    nc = seq // L
    return pl.pallas_call(
        functools.partial(_mlstm_kernel, chunks_per_seq=nc),
        grid=(nt // L,),
        in_specs=[pl.BlockSpec((L, 4 * B_WIDTH), lambda i: (i, 0)),
                  pl.BlockSpec((SUBLANES, 4 * B_WIDTH), lambda i: (jnp.maximum(i * (L // SUBLANES) - 1, 0), 0)),
                  pl.BlockSpec((L, LANES), lambda i: (i, 0)),
                  pl.BlockSpec((1, SUBLANES, L), lambda i: (i // nc, 0, i % nc)),
                  _full(conv_w.shape), _full(conv_b.shape), _full(bif_row.shape), _full(bif_col.shape),
                  _full(hn_g.shape), _full(ltri.shape), _full(utri.shape)],
        out_specs=pl.BlockSpec((L, B_WIDTH), lambda i: (i, 0)),
        out_shape=jax.ShapeDtypeStruct((nt, B_WIDTH), F32),
        scratch_shapes=[pltpu.VMEM((B_HEADS, B_HEAD_DIM, B_HEAD_DIM), F32),
                        pltpu.VMEM((B_HEADS, 1, B_HEAD_DIM), F32),
                        pltpu.VMEM((B_HEADS, 1, 1), F32)],
        compiler_params=_params(),
        name="mlstm",
    )(p_b, p_b, p_if, g_t, conv_w, conv_b, bif_row, bif_col, hn_g, ltri, utri)


def _swa_kernel(q_ref, kvc_ref, kvp_ref, qn_ref, kn_ref, sink_ref, o_ref, *, blocks_per_seq):
    i = pl.program_id(0)
    first = (i % blocks_per_seq) == 0
    bk = C_BLOCK
    d = C_HEAD_DIM
    q = q_ref[...]
    kvc = kvc_ref[...]
    kvp = kvp_ref[...]
    kb = jnp.concatenate([kvp[:, :C_KV_WIDTH], kvc[:, :C_KV_WIDTH]], axis=0)
    vb = jnp.concatenate([kvp[:, C_KV_WIDTH:], kvc[:, C_KV_WIDTH:]], axis=0)
    qi = lax.broadcasted_iota(jnp.int32, (bk, 2 * bk), 0)
    kj = lax.broadcasted_iota(jnp.int32, (bk, 2 * bk), 1)
    dist = qi + bk - kj
    valid = (dist >= 0) & (dist < WINDOW) & (kj >= jnp.where(first, bk, 0))
    distf = dist.astype(F32)
    qn = qn_ref[...]
    kn = kn_ref[...]
    group = C_Q_HEADS // C_KV_HEADS
    for kvh in range(C_KV_HEADS):
        ksl = slice(kvh * d, (kvh + 1) * d)
        kh = kb[:, ksl]
        kh = kh * lax.rsqrt(jnp.mean(kh * kh, axis=-1, keepdims=True) + EPS) * kn
        vh = vb[:, ksl]
        for j in range(group):
            h = kvh * group + j
            sl = slice(h * d, (h + 1) * d)
            qh = q[:, sl]
            qh = qh * lax.rsqrt(jnp.mean(qh * qh, axis=-1, keepdims=True) + EPS) * qn * (d ** -0.5)
            slope = float(2.0 ** (-8.0 * (h + 1) / C_Q_HEADS))
            s = jnp.where(valid, _mm_nt(qh, kh) - slope * distf, -jnp.inf)
            sink = sink_ref[:, h:h + 1]
            m = jnp.maximum(jnp.max(s, axis=-1, keepdims=True), sink)
            p = jnp.exp(s - m)
            den = jnp.sum(p, axis=-1, keepdims=True) + jnp.exp(sink - m)
            o_ref[:, sl] = _mm(p / den, vh)


def _swa(p_cq, p_ckv, qn_g, kn_g, sink_row, seq):
    nt = p_cq.shape[0]
    bk = C_BLOCK
    nb = seq // bk
    return pl.pallas_call(
        functools.partial(_swa_kernel, blocks_per_seq=nb),
        grid=(nt // bk,),
        in_specs=[pl.BlockSpec((bk, C_WIDTH), lambda i: (i, 0)),
                  pl.BlockSpec((bk, 2 * C_KV_WIDTH), lambda i: (i, 0)),
                  pl.BlockSpec((bk, 2 * C_KV_WIDTH), lambda i: (jnp.maximum(i - 1, 0), 0)),
                  _full(qn_g.shape), _full(kn_g.shape), _full(sink_row.shape)],
        out_specs=pl.BlockSpec((bk, C_WIDTH), lambda i: (i, 0)),
        out_shape=jax.ShapeDtypeStruct((nt, C_WIDTH), F32),
        compiler_params=_params(),
        name="swa",
    )(p_cq, p_ckv, p_ckv, qn_g, kn_g, sink_row)


def _merge_kernel(x_ref, gt_ref, pg_ref, ya_ref, r_ref, k_ref, v_ref, g_ref, yb_ref, yc_ref,
                  hsum_ref, lng_ref, lnb_ref, rk_ref, wa_ref, wb_ref, wc_ref, wo_ref, o_ref):
    hsum = hsum_ref[...]
    inv_n = 1.0 / A_HEAD_DIM
    y = ya_ref[...]
    mu = _mm32(y, hsum) * inv_n
    dlt = y - mu
    var = _mm32(dlt * dlt, hsum) * inv_n
    yn = dlt * lax.rsqrt(var + A_LN_EPS) * lng_ref[...] + lnb_ref[...]
    bonus = _mm32(r_ref[...] * k_ref[...] * rk_ref[...], hsum) * v_ref[...]
    y_a = (yn + bonus) * g_ref[...]
    d = x_ref.shape[1]
    pg = pg_ref[...]
    merged = (jax.nn.sigmoid(pg[:, :d]) * _mm(y_a, wa_ref[...])
              + jax.nn.sigmoid(pg[:, d:2 * d]) * _mm(yb_ref[...], wb_ref[...])
              + jax.nn.sigmoid(pg[:, 2 * d:]) * _mm(yc_ref[...], wc_ref[...]))
    o_ref[...] = x_ref[...] + gt_ref[0] * _mm(merged, wo_ref[...])


def _merge(x2, gt, p_g, y_scan, r, k, v, g, y_b, y_c, hsum, lng, lnb, rk, wa, wb, wc, wo, seq):
    nt, d = x2.shape
    tm = ROW_TILE
    tps = seq // tm
    row = lambda w: pl.BlockSpec((tm, w), lambda i: (i, 0))
    consts = (hsum, lng, lnb, rk, wa, wb, wc, wo)
    return pl.pallas_call(
        _merge_kernel,
        grid=(nt // tm,),
        in_specs=[row(d), pl.BlockSpec((1, 1, d), lambda i: (i // tps, 0, 0)), row(3 * d)]
                 + [row(A_WIDTH)] * 5 + [row(B_WIDTH), row(C_WIDTH)] + [_full(c.shape) for c in consts],
        out_specs=row(d),
        out_shape=jax.ShapeDtypeStruct((nt, d), F32),
        compiler_params=_params(),
        name="merge",
    )(x2, gt, p_g, y_scan, r, k, v, g, y_b, y_c, *consts)


def _route_kernel(x_ref, g_ref, sc_ref, sh_ref, wr_ref, br_ref, h_ref, idx_ref, wgt_ref):
    x = x_ref[...]
    y = x * lax.rsqrt(jnp.mean(x * x, axis=-1, keepdims=True) + EPS) * g_ref[...]
    h = y * (1.0 + sc_ref[0]) + sh_ref[0]
    h_ref[...] = h
    lg = _mm32(h, wr_ref[...]) + br_ref[...]
    lane = lax.broadcasted_iota(jnp.int32, lg.shape, 1)
    lanef = lane.astype(F32)
    ninf = -jnp.inf
    big = float(LANES)

    def first_argmax(vals, vmax):
        return jnp.min(jnp.where(vals == vmax, lanef, big), axis=-1, keepdims=True)

    gl = jnp.where(lane < N_GROUPS, lg, ninf)
    gmax = jnp.max(gl, axis=-1, keepdims=True)
    g_top = 1.0 / jnp.sum(jnp.exp(gl - gmax), axis=-1, keepdims=True)
    g_idx = first_argmax(gl, gmax)
    lo = N_GROUPS + EXPERTS_PER_GROUP * g_idx
    el = jnp.where((lanef >= lo) & (lanef < lo + EXPERTS_PER_GROUP), lg, ninf)
    e1 = jnp.max(el, axis=-1, keepdims=True)
    z = jnp.sum(jnp.exp(el - e1), axis=-1, keepdims=True)
    i1 = first_argmax(el, e1)
    el2 = jnp.where(lanef == i1, ninf, el)
    e2 = jnp.max(el2, axis=-1, keepdims=True)
    i2 = first_argmax(el2, e2)
    p1 = 1.0 / z
    p2 = jnp.exp(e2 - e1) / z
    w1 = g_top * p1 / (p1 + p2)
    w2 = g_top * p2 / (p1 + p2)
    idx = jnp.where(lane == 0, i1 - N_GROUPS, jnp.where(lane == 1, i2 - N_GROUPS, 0.0))
    idx_ref[...] = idx.astype(jnp.int32)
    wgt_ref[...] = jnp.where(lane == 0, w1, jnp.where(lane == 1, w2, 0.0))


def _route(x2, g, sc, sh, w_r, b_r, seq):
    nt, d = x2.shape
    tm = ROW_TILE
    tps = seq // tm
    row = lambda w: pl.BlockSpec((tm, w), lambda i: (i, 0))
    mod = pl.BlockSpec((1, 1, d), lambda i: (i // tps, 0, 0))
    return pl.pallas_call(
        _route_kernel,
        grid=(nt // tm,),
        in_specs=[row(d), _full((1, d)), mod, mod, _full(w_r.shape), _full(b_r.shape)],
        out_specs=[row(d), row(LANES), row(LANES)],
        out_shape=[jax.ShapeDtypeStruct((nt, d), F32), jax.ShapeDtypeStruct((nt, LANES), jnp.int32),
                   jax.ShapeDtypeStruct((nt, LANES), F32)],
        compiler_params=_params(),
        name="route",
    )(x2, g, sc, sh, w_r, b_r)


def _gather_rows(idx_ref, n_rows, src_hbm, dst, sem):
    def body(r, carry):
        pltpu.make_async_copy(src_hbm.at[pl.ds(idx_ref[0, 0, r], 1), :], dst.at[pl.ds(r, 1), :], sem).start()
        return carry
    lax.fori_loop(0, n_rows, body, 0)


def _wait_rows(n_rows, src_hbm, dst, sem):
    def body(r, carry):
        pltpu.make_async_copy(src_hbm.at[pl.ds(0, 1), :], dst.at[pl.ds(r, 1), :], sem).wait()
        return carry
    lax.fori_loop(0, n_rows, body, 0)


def _expert_kernel(bexp_ref, tok_ref, tok_next_ref, h_hbm, win_ref, wout_ref, o_ref, buf, sem):
    del bexp_ref
    i = pl.program_id(0)
    n = pl.num_programs(0)
    slot = i % 2
    rows = MOE_BLOCK

    @pl.when(i == 0)
    def _():
        _gather_rows(tok_ref, rows, h_hbm, buf.at[0], sem.at[0])

    @pl.when(i + 1 < n)
    def _():
        _gather_rows(tok_next_ref, rows, h_hbm, buf.at[1 - slot], sem.at[1 - slot])

    _wait_rows(rows, h_hbm, buf.at[slot], sem.at[slot])
    xs = buf[slot]
    gu = _mm(xs, win_ref[0])
    gate = gu[:, :D_EXPERT]
    up = gu[:, D_EXPERT:]
    o_ref[...] = _mm(gate * jax.nn.sigmoid(gate) * up, wout_ref[0])


def _experts(block_exp, slot_tok, h2, w_in_e, w_out_e):
    n_blocks = block_exp.shape[0]
    rows = MOE_BLOCK
    d = h2.shape[1]
    tok3 = slot_tok.reshape(n_blocks, 1, rows)
    smem_blk = lambda f: pl.BlockSpec((1, 1, rows), f, memory_space=pltpu.SMEM)
    grid_spec = pltpu.PrefetchScalarGridSpec(
        num_scalar_prefetch=1,
        grid=(n_blocks,),
        in_specs=[smem_blk(lambda i, be: (i, 0, 0)),
                  smem_blk(lambda i, be: (jnp.minimum(i + 1, n_blocks - 1), 0, 0)),
                  pl.BlockSpec(memory_space=pl.ANY),
                  pl.BlockSpec((1, d, 2 * D_EXPERT), lambda i, be: (be[i], 0, 0)),
                  pl.BlockSpec((1, D_EXPERT, d), lambda i, be: (be[i], 0, 0))],
        out_specs=pl.BlockSpec((rows, d), lambda i, be: (i, 0)),
        scratch_shapes=[pltpu.VMEM((2, rows, d), F32), pltpu.SemaphoreType.DMA((2,))],
    )
    return pl.pallas_call(
        _expert_kernel,
        grid_spec=grid_spec,
        out_shape=jax.ShapeDtypeStruct((n_blocks * rows, d), F32),
        compiler_params=_params(),
        name="experts",
    )(block_exp, tok3, tok3, h2, w_in_e, w_out_e)


def _combine_kernel(pos_ref, pos_next_ref, x_ref, gt_ref, wgt_ref, yb_hbm, o_ref, buf, sem):
    i = pl.program_id(0)
    n = pl.num_programs(0)
    slot = i % 2
    rows = TOP_K * MOE_BLOCK

    @pl.when(i == 0)
    def _():
        _gather_rows(pos_ref, rows, yb_hbm, buf.at[0], sem.at[0])

    @pl.when(i + 1 < n)
    def _():
        _gather_rows(pos_next_ref, rows, yb_hbm, buf.at[1 - slot], sem.at[1 - slot])

    _wait_rows(rows, yb_hbm, buf.at[slot], sem.at[slot])
    wgt = wgt_ref[...]
    y = buf[slot, :MOE_BLOCK, :] * wgt[:, 0:1] + buf[slot, MOE_BLOCK:, :] * wgt[:, 1:2]
    o_ref[...] = x_ref[...] + gt_ref[0] * y


def _combine(pos, x2, gt, wgt, yb, seq):
    nt, d = x2.shape
    tm = MOE_BLOCK
    tps = seq // tm
    n_tiles = nt // tm
    pos3 = pos.reshape(n_tiles, tm, TOP_K).transpose(0, 2, 1).reshape(n_tiles, 1, TOP_K * tm)
    smem_blk = lambda f: pl.BlockSpec((1, 1, TOP_K * tm), f, memory_space=pltpu.SMEM)
    row = lambda w: pl.BlockSpec((tm, w), lambda i: (i, 0))
    return pl.pallas_call(
        _combine_kernel,
        grid=(n_tiles,),
        in_specs=[smem_blk(lambda i: (i, 0, 0)),
                  smem_blk(lambda i: (jnp.minimum(i + 1, n_tiles - 1), 0, 0)),
                  row(d), pl.BlockSpec((1, 1, d), lambda i: (i // tps, 0, 0)), row(LANES),
                  pl.BlockSpec(memory_space=pl.ANY)],
        out_specs=row(d),
        out_shape=jax.ShapeDtypeStruct((nt, d), F32),
        scratch_shapes=[pltpu.VMEM((2, TOP_K * tm, d), F32), pltpu.SemaphoreType.DMA((2,))],
        compiler_params=_params(),
        name="combine",
    )(pos3, pos3, x2, gt, wgt, yb)


def _dispatch_plan(expert_idx, n_tokens):
    na = n_tokens * TOP_K
    a_exp = expert_idx.reshape(-1)
    onehot = (a_exp[:, None] == jnp.arange(N_EXPERTS, dtype=jnp.int32)[None, :]).astype(jnp.int32)
    csum = jnp.cumsum(onehot, axis=0)
    rank = jnp.sum(csum * onehot, axis=1) - 1
    counts = csum[-1]
    padded = (counts + MOE_BLOCK - 1) // MOE_BLOCK * MOE_BLOCK
    pad_ends = jnp.cumsum(padded)
    pad_starts = pad_ends - padded
    dest = pad_starts[a_exp] + rank
    n_blocks = -(-na // MOE_BLOCK) + N_EXPERTS
    slot_tok = jnp.zeros((n_blocks * MOE_BLOCK,), jnp.int32).at[dest].set(
        jnp.arange(na, dtype=jnp.int32) // TOP_K)
    block_exp = jnp.minimum(
        jnp.searchsorted(pad_ends, jnp.arange(n_blocks, dtype=jnp.int32) * MOE_BLOCK, side='right'),
        N_EXPERTS - 1).astype(jnp.int32)
    return block_exp, slot_tok, dest.reshape(n_tokens, TOP_K).astype(jnp.int32)


def _pad_cols(w, width):
    return jnp.pad(w, ((0, 0), (0, width - w.shape[1])))


def kernel(x, c, w_ada, b_ada, norm1_g, norm2_g, w_in, mu_a, w0_a, w2_a, a0_a, a2_a, g2_a, kk_a, ka_a, rk_a,
           lnx_g_a, lnx_b_a, conv_w_b, conv_b_b, bi_b, bf_b, hn_g_b, qn_g_c, kn_g_c, sink_c, w_br_a, w_br_b,
           w_br_c, w_out, w_rg, b_rg, w_re, b_re, w_e_in, w_e_out):
    bsz, seq, d = x.shape
    nt = bsz * seq
    depth = w_in.shape[0]
    assert bsz * A_HEADS * 4 == LANES, "the RWKV recurrence packs (batch, head, 4) into the lane axis"
    assert seq % ROW_TILE == 0 and seq % B_CHUNK == 0 and seq % SCAN_STEPS == 0

    mod = _ada_mod(c, w_ada, b_ada)
    hsum = jnp.kron(jnp.eye(A_HEADS, dtype=F32), jnp.ones((A_HEAD_DIM, A_HEAD_DIM), F32))
    tri = jnp.tril(jnp.ones((B_CHUNK, B_CHUNK), F32))

    o_b = 1024
    o_if = o_b + 4 * B_WIDTH
    o_cq = o_if + 2 * B_HEADS
    o_ckv = o_cq + C_WIDTH
    o_g = o_ckv + 2 * C_KV_WIDTH

    x2 = x.reshape(nt, d)
    for l in range(depth):
        m = mod[l].reshape(bsz, 6, 1, d)
        sh1, sc1, gt1, sh2, sc2, gt2 = (m[:, i] for i in range(6))
        wl = w_in[l]
        weights = [wl[:, o_g:].astype(BF16), wl[:, :o_b].astype(BF16), wl[:, o_b:o_if].astype(BF16),
                   wl[:, o_cq:o_ckv].astype(BF16), wl[:, o_ckv:o_g].astype(BF16),
                   _pad_cols(wl[:, o_if:o_cq], LANES).astype(BF16)]
        p_g, p_a, p_b, p_cq, p_ckv, p_if = _norm_proj(x2, norm1_g[l][None], sc1, sh1, weights, seq)

        wwa = jnp.zeros((128, 512), F32).at[:64, :256].set(w2_a[l]).at[64:, 256:].set(a2_a[l])
        r, dec, k2, v, kk, bvec, g = _rwkv_prep(p_a, mu_a[l][None], wwa, w0_a[l][None], a0_a[l][None], g2_a[l],
                                                kk_a[l][None], ka_a[l][None], hsum, seq)
        y_scan = _rwkv_scan(*(_expand_key(t, bsz, seq) for t in (dec, kk, bvec, k2, r)),
                            _permute_value(v, bsz, seq))
        y_scan = _unpermute_value(y_scan, bsz, seq)

        g_t = p_if[:, :SUBLANES].reshape(bsz, seq, SUBLANES).transpose(0, 2, 1)
        bif = jnp.concatenate([bi_b[l], bf_b[l]])
        y_b = _mlstm(p_b, p_if, g_t, conv_w_b[l], conv_b_b[l][None], _pad_cols(bif[None], LANES), bif[:, None],
                     hn_g_b[l].reshape(1, B_WIDTH), tri, tri.T, seq)

        y_c = _swa(p_cq, p_ckv, qn_g_c[l][None], kn_g_c[l][None], _pad_cols(sink_c[l][None], LANES), seq)

        x2 = _merge(x2, gt1, p_g, y_scan, r, k2, v, g, y_b, y_c, hsum, lnx_g_a[l][None], lnx_b_a[l][None],
                    rk_a[l].reshape(1, A_WIDTH), w_br_a[l].astype(BF16), w_br_b[l].astype(BF16),
                    w_br_c[l].astype(BF16), w_out[l].astype(BF16), seq)

        w_r = _pad_cols(jnp.concatenate([w_rg[l], w_re[l]], axis=1), LANES)
        b_r = _pad_cols(jnp.concatenate([b_rg[l], b_re[l]])[None], LANES)
        h2, ridx, rwgt = _route(x2, norm2_g[l][None], sc2, sh2, w_r, b_r, seq)
        block_exp, slot_tok, pos = _dispatch_plan(ridx[:, :TOP_K], nt)
        yb = _experts(block_exp, slot_tok, h2, w_e_in[l].astype(BF16), w_e_out[l].astype(BF16))
        x2 = _combine(pos, x2, gt2, rwgt, yb, seq)
    return x2.reshape(bsz, seq, d)
```

```python
import functools

import numpy as np
import jax
import jax.numpy as jnp
from jax import lax
from jax.experimental import pallas as pl
from jax.experimental.pallas import tpu as pltpu

F32 = jnp.float32
BF16 = jnp.bfloat16
HIGHEST = lax.Precision.HIGHEST

A_HEADS = 4
A_HEAD_DIM = 64
A_WIDTH = 256
A_DECAY_LORA = 64
A_ICL_LORA = 64
A_LN_EPS = 64e-5
B_HEADS = 4
B_HEAD_DIM = 64
B_WIDTH = 256
B_CHUNK = 128
C_Q_HEADS = 8
C_KV_HEADS = 2
C_HEAD_DIM = 64
C_WIDTH = 512
C_KV_WIDTH = 128
WINDOW = 128
C_BLOCK = 128
N_GROUPS = 4
EXPERTS_PER_GROUP = 8
N_EXPERTS = 32
TOP_K = 2
D_EXPERT = 512
MOE_BLOCK = 128
EPS = 1e-6

LANES = 128
SUBLANES = 8
VMEM_LIMIT = 56 * 1024 * 1024

ROW_TILE = 256
SCAN_STEPS = 16
ADA_COL_TILE = 1536


def _mm(a, b):
    return jnp.dot(a.astype(BF16), b.astype(BF16), preferred_element_type=F32)


def _mm32(a, b):
    return jnp.dot(a, b, precision=HIGHEST, preferred_element_type=F32)


def _mm_nt(a, b):
    return lax.dot_general(a.astype(BF16), b.astype(BF16), (((1,), (1,)), ((), ())),
                           preferred_element_type=F32)


def _mm_tn(a, b):
    return lax.dot_general(a.astype(BF16), b.astype(BF16), (((0,), (0,)), ((), ())),
                           preferred_element_type=F32)


def _log_sigmoid(x):
    return jnp.minimum(x, 0.0) - jnp.log(1.0 + jnp.exp(-jnp.abs(x)))


def _params(n_axes=1):
    return pltpu.CompilerParams(dimension_semantics=("arbitrary",) * n_axes,
                                vmem_limit_bytes=VMEM_LIMIT)


def _full(shape):
    nd = len(shape)
    return pl.BlockSpec(shape, lambda *_: (0,) * nd)


def _ada_kernel(c_ref, w_ref, b_ref, o_ref):
    c = c_ref[...]
    cond = c * jax.nn.sigmoid(c)
    o_ref[0] = _mm(cond, w_ref[0]) + b_ref[0]


def _ada_mod(c, w_ada, b_ada):
    depth, d, n = w_ada.shape
    bsz = c.shape[0]
    tn = ADA_COL_TILE
    return pl.pallas_call(
        _ada_kernel,
        grid=(depth, n // tn),
        in_specs=[pl.BlockSpec((bsz, d), lambda l, j: (0, 0)),
                  pl.BlockSpec((1, d, tn), lambda l, j: (l, 0, j)),
                  pl.BlockSpec((1, 1, tn), lambda l, j: (l, 0, j))],
        out_specs=pl.BlockSpec((1, bsz, tn), lambda l, j: (l, 0, j)),
        out_shape=jax.ShapeDtypeStruct((depth, bsz, n), F32),
        compiler_params=_params(2),
        name="ada_mod",
    )(c, w_ada, b_ada.reshape(depth, 1, n))


def _start_row_gather(idx_ref, n_rows, src_hbm, dst, sem):
    for r in range(n_rows):
        pltpu.make_async_copy(src_hbm.at[pl.ds(idx_ref[0, 0, r], 1), :], dst.at[pl.ds(r, 1), :],
                              sem).start(priority=r % 2)


def _wait_row_gather(n_rows, src_hbm, dst, sem):
    pltpu.make_async_copy(src_hbm.at[pl.ds(0, n_rows), :], dst, sem).wait()


def _pipelined_row_gather(idx_ref, idx_next_ref, n_rows, src_hbm, buf, sem):
    i = pl.program_id(0)
    slot = i % 2

    @pl.when(i == 0)
    def _():
        _start_row_gather(idx_ref, n_rows, src_hbm, buf.at[0], sem.at[0])

    _start_row_gather(idx_next_ref, n_rows, src_hbm, buf.at[1 - slot], sem.at[1 - slot])
    _wait_row_gather(n_rows, src_hbm, buf.at[slot], sem.at[slot])

    @pl.when(i == pl.num_programs(0) - 1)
    def _():
        _wait_row_gather(n_rows, src_hbm, buf.at[1 - slot], sem.at[1 - slot])

    return slot


def _moe_residual(pos_ref, pos_next_ref, x_ref, gt_ref, wgt_ref, yb_hbm, buf, sem):
    tm = x_ref.shape[0]
    slot = _pipelined_row_gather(pos_ref, pos_next_ref, TOP_K * tm, yb_hbm, buf, sem)
    wgt = wgt_ref[...]
    y = buf[slot, :tm, :] * wgt[:, 0:1] + buf[slot, tm:, :] * wgt[:, 1:2]
    return x_ref[...] + gt_ref[0] * y


def _moe_residual_specs(pos, tm, n_tiles, d, tps):
    pos3 = pos.reshape(n_tiles, tm, TOP_K).transpose(0, 2, 1).reshape(n_tiles, 1, TOP_K * tm)
    smem_blk = lambda f: pl.BlockSpec((1, 1, TOP_K * tm), f, memory_space=pltpu.SMEM)
    row = lambda w: pl.BlockSpec((tm, w), lambda i: (i, 0))
    specs = [smem_blk(lambda i: (i, 0, 0)),
             smem_blk(lambda i: (jnp.minimum(i + 1, n_tiles - 1), 0, 0)),
             row(d), pl.BlockSpec((1, 1, d), lambda i: (i // tps, 0, 0)), row(LANES),
             pl.BlockSpec(memory_space=pl.ANY)]
    scratch = [pltpu.VMEM((2, TOP_K * tm, d), F32), pltpu.SemaphoreType.DMA((2,))]
    return pos3, specs, scratch


def _norm_proj_kernel(*refs, n_proj, moe):
    if moe:
        x = _moe_residual(*refs[:6], *refs[-2:])
        refs = refs[6:-2]
        refs[-1][...] = x
        refs = refs[:-1]
    else:
        x = refs[0][...]
        refs = refs[1:]
    g_ref, sc_ref, sh_ref = refs[:3]
    y = x * lax.rsqrt(jnp.mean(x * x, axis=-1, keepdims=True) + EPS) * g_ref[...]
    h = (y * (1.0 + sc_ref[0]) + sh_ref[0]).astype(BF16)
    for w_ref, o_ref in zip(refs[3:3 + n_proj], refs[3 + n_proj:]):
        o_ref[...] = jnp.dot(h, w_ref[...], preferred_element_type=F32)


def _norm_proj(x2, g, sc, sh, weights, seq, moe=None):
    nt, d = x2.shape
    tm = ROW_TILE
    tps = seq // tm
    n_tiles = nt // tm
    row = lambda w: pl.BlockSpec((tm, w), lambda i: (i, 0))
    mod = pl.BlockSpec((1, 1, d), lambda i: (i // tps, 0, 0))
    in_specs = [_full((1, d)), mod, mod] + [_full(w.shape) for w in weights]
    out_specs = [row(w.shape[1]) for w in weights]
    out_shape = [jax.ShapeDtypeStruct((nt, w.shape[1]), F32) for w in weights]
    if moe is None:
        args, head_specs, scratch = [x2], [row(d)], []
    else:
        pos, gt, wgt, yb = moe
        pos3, head_specs, scratch = _moe_residual_specs(pos, tm, n_tiles, d, tps)
        args = [pos3, pos3, x2, gt, wgt, yb]
        out_specs.append(row(d))
        out_shape.append(jax.ShapeDtypeStruct((nt, d), F32))
    outs = pl.pallas_call(
        functools.partial(_norm_proj_kernel, n_proj=len(weights), moe=moe is not None),
        grid=(n_tiles,),
        in_specs=head_specs + in_specs,
        out_specs=out_specs,
        out_shape=out_shape,
        scratch_shapes=scratch,
        compiler_params=_params(),
        name="norm_proj_moe" if moe is not None else "norm_proj",
    )(*args, g, sc, sh, *weights)
    outs = list(outs)
    return outs if moe is None else [outs[-1]] + outs[:-1]


def _rwkv_prep_kernel(p_ref, prev_ref, mu_ref, wwa_ref, w0_ref, a0_ref, g2_ref, kkw_ref, kaw_ref, hsum_ref,
                      r_out, w_out, k_out, v_out, kk_out, b_out, g_out, *, tiles_per_seq):
    i = pl.program_id(0)
    p = p_ref[...]
    first = (i % tiles_per_seq) == 0
    prev_row = jnp.where(first, 0.0, prev_ref[SUBLANES - 1:SUBLANES, :])
    row = lax.broadcasted_iota(jnp.int32, p.shape, 0)
    shifted = jnp.where(row == 0, prev_row, pltpu.roll(p, 1, 0))
    pa = p + mu_ref[...] * (shifted - p)
    aw = A_WIDTH
    o_lora = 3 * aw
    o_glo = o_lora + A_DECAY_LORA + A_ICL_LORA
    r = pa[:, 0:aw]
    k = pa[:, aw:2 * aw]
    v = pa[:, 2 * aw:o_lora]
    slab = pa[:, o_lora:o_glo]
    glo = pa[:, o_glo:]
    lane = lax.broadcasted_iota(jnp.int32, slab.shape, 1)
    z = jnp.where(lane < A_DECAY_LORA, jnp.tanh(slab), slab)
    wa = _mm32(z, wwa_ref[...])
    w_log = _log_sigmoid(w0_ref[...] + wa[:, :aw]) - 0.5
    decay = jnp.exp(-jnp.exp(w_log))
    a = jax.nn.sigmoid(a0_ref[...] + wa[:, aw:])
    g = _mm32(jax.nn.sigmoid(glo), g2_ref[...])
    kk = k * kkw_ref[...]
    ss = _mm32(kk * kk, hsum_ref[...])
    kk = kk / jnp.maximum(jnp.sqrt(ss), 1e-12)
    r_out[...] = r
    w_out[...] = decay
    k_out[...] = k * (1.0 + (a - 1.0) * kaw_ref[...])
    v_out[...] = v
    kk_out[...] = kk
    b_out[...] = kk * a
    g_out[...] = g


def _rwkv_prep(p_a, mu, wwa, w0, a0, g2, kkw, kaw, hsum, seq):
    nt, wd = p_a.shape
    tm = ROW_TILE
    tps = seq // tm
    row = lambda w: pl.BlockSpec((tm, w), lambda i: (i, 0))
    prev = pl.BlockSpec((SUBLANES, wd), lambda i: (jnp.maximum(i * (tm // SUBLANES) - 1, 0), 0))
    outs = [jax.ShapeDtypeStruct((nt, A_WIDTH), F32)] * 7
    return pl.pallas_call(
        functools.partial(_rwkv_prep_kernel, tiles_per_seq=tps),
        grid=(nt // tm,),
        in_specs=[row(wd), prev, _full(mu.shape), _full(wwa.shape), _full(w0.shape), _full(a0.shape),
                  _full(g2.shape), _full(kkw.shape), _full(kaw.shape), _full(hsum.shape)],
        out_specs=[row(A_WIDTH)] * 7,
        out_shape=outs,
        compiler_params=_params(),
        name="rwkv_prep",
    )(p_a, p_a, mu, wwa, w0, a0, g2, kkw, kaw, hsum)


def _rwkv_scan_kernel(w_ref, kk_ref, b_ref, k_ref, r_ref, v_ref, y_ref, s_ref, *, steps):
    @pl.when(pl.program_id(0) == 0)
    def _():
        s_ref[...] = jnp.zeros_like(s_ref)

    def tree(terms):
        while len(terms) > 1:
            terms = [terms[i] + terms[i + 1] for i in range(0, len(terms), 2)]
        return terms[0]

    def step(t, carry):
        for ih in range(2):
            base = ih * A_HEAD_DIM
            v_t = v_ref[t, ih * SUBLANES:(ih + 1) * SUBLANES, :]
            sa = tree([s_ref[base + j] * kk_ref[t, j:j + 1, :] for j in range(A_HEAD_DIM)])
            ys = []
            for j in range(A_HEAD_DIM):
                s = (s_ref[base + j] * w_ref[t, j:j + 1, :] - sa * b_ref[t, j:j + 1, :]
                     + v_t * k_ref[t, j:j + 1, :])
                s_ref[base + j] = s
                ys.append(s * r_ref[t, j:j + 1, :])
            y_ref[t, ih * SUBLANES:(ih + 1) * SUBLANES, :] = tree(ys)
        return carry

    lax.fori_loop(0, steps, step, 0)


def _rwkv_scan(w_e, kk_e, b_e, k_e, r_e, v_p):
    seq = w_e.shape[0]
    tc = SCAN_STEPS
    key = pl.BlockSpec((tc, A_HEAD_DIM, LANES), lambda i: (i, 0, 0))
    val = pl.BlockSpec((tc, 2 * SUBLANES, LANES), lambda i: (i, 0, 0))
    return pl.pallas_call(
        functools.partial(_rwkv_scan_kernel, steps=tc),
        grid=(seq // tc,),
        in_specs=[key] * 5 + [val],
        out_specs=val,
        out_shape=jax.ShapeDtypeStruct((seq, 2 * SUBLANES, LANES), F32),
        scratch_shapes=[pltpu.VMEM((2 * A_HEAD_DIM, SUBLANES, LANES), F32)],
        compiler_params=_params(),
        name="rwkv_scan",
    )(w_e, kk_e, b_e, k_e, r_e, v_p)


def _expand_key(x, bsz, seq):
    x = x.reshape(bsz, seq, A_HEADS, A_HEAD_DIM).transpose(1, 3, 0, 2)
    x = jnp.broadcast_to(x[..., None], (seq, A_HEAD_DIM, bsz, A_HEADS, 4))
    return x.reshape(seq, A_HEAD_DIM, LANES)


def _permute_value(x, bsz, seq):
    x = x.reshape(bsz, seq, A_HEADS, 2, SUBLANES, 4).transpose(1, 3, 4, 0, 2, 5)
    return x.reshape(seq, 2 * SUBLANES, LANES)


def _unpermute_value(y, bsz, seq):
    y = y.reshape(seq, 2, SUBLANES, bsz, A_HEADS, 4).transpose(3, 0, 4, 1, 2, 5)
    return y.reshape(bsz * seq, A_WIDTH)


def _mlstm_kernel(pb_ref, prev_ref, pif_ref, gt_ref, cw_ref, cb_ref, bifr_ref, bifc_ref, hng_ref,
                  ltri_ref, utri_ref, o_ref, c_ref, n_ref, m_ref, *, chunks_per_seq):
    i = pl.program_id(0)
    first = (i % chunks_per_seq) == 0

    @pl.when(first)
    def _():
        c_ref[...] = jnp.zeros_like(c_ref)
        n_ref[...] = jnp.zeros_like(n_ref)
        m_ref[...] = jnp.zeros_like(m_ref)

    L = B_CHUNK
    pb = pb_ref[...]
    x = pb[:, :2 * B_WIDTH]
    prev = jnp.where(first, 0.0, prev_ref[:, :2 * B_WIDTH])
    cw = cw_ref[...]
    taps = cw.shape[0]
    acc = x * cw[taps - 1:taps, :] + cb_ref[...]
    r8 = lax.broadcasted_iota(jnp.int32, prev.shape, 0)
    for s in range(1, taps):
        xs = pltpu.roll(x, s, 0)
        head = jnp.where(r8 < s, pltpu.roll(prev, s, 0), xs[:SUBLANES])
        xs = jnp.concatenate([head, xs[SUBLANES:]], axis=0)
        acc = acc + xs * cw[taps - 1 - s:taps - s, :]
    qk = acc * jax.nn.sigmoid(acc)
    q = qk[:, :B_WIDTH]
    k = qk[:, B_WIDTH:] * (B_HEAD_DIM ** -0.5)
    v = pb[:, 2 * B_WIDTH:3 * B_WIDTH]
    o = pb[:, 3 * B_WIDTH:]

    pif = pif_ref[...] + bifr_ref[...]
    lane = lax.broadcasted_iota(jnp.int32, pif.shape, 1)
    logf_c = jnp.where(lane >= B_HEADS, _log_sigmoid(pif), 0.0)
    f_c = _mm32(ltri_ref[...], logf_c)
    g = gt_ref[0] + bifc_ref[...]
    row = lax.broadcasted_iota(jnp.int32, g.shape, 0)
    logf_r = jnp.where(row >= B_HEADS, _log_sigmoid(g), 0.0)
    f_r = _mm32(logf_r, utri_ref[...])

    ti = lax.broadcasted_iota(jnp.int32, (L, L), 0)
    si = lax.broadcasted_iota(jnp.int32, (L, L), 1)
    causal = si <= ti

    W = B_WIDTH
    lane_head = lax.broadcasted_iota(jnp.int32, (L, W), 1) // B_HEAD_DIM
    lane_head_row = lax.broadcasted_iota(jnp.int32, (1, W), 1) // B_HEAD_DIM
    row_head_col = lax.broadcasted_iota(jnp.int32, (W, 1), 0) // B_HEAD_DIM
    same_head = (lax.broadcasted_iota(jnp.int32, (W, W), 0) // B_HEAD_DIM
                 == lax.broadcasted_iota(jnp.int32, (W, W), 1) // B_HEAD_DIM)
    c0 = c_ref[...]
    n0 = n_ref[...]
    q_c = _mm_nt(q, c0)
    q_n = q * n0
    inv_d = 1.0 / B_HEAD_DIM

    hh = jnp.zeros((L, W), F32)
    w_all = jnp.zeros((L, W), F32)
    cd_lane = jnp.zeros((1, W), F32)
    cd_rows = jnp.zeros((W, 1), F32)
    for h in range(B_HEADS):
        hm = lane_head == h
        fc = f_c[:, B_HEADS + h:B_HEADS + h + 1]
        igc = pif[:, h:h + 1]
        fr = f_r[B_HEADS + h:B_HEADS + h + 1, :]
        igr = g[h:h + 1, :]
        f_last = fr[:, L - 1:L]
        m0 = m_ref[h]

        d = jnp.where(causal, fc - fr + igr, -jnp.inf)
        g_inter = fc + m0
        m_t = jnp.maximum(g_inter, jnp.max(d, axis=-1, keepdims=True))
        sw = _mm_nt(jnp.where(hm, q, 0.0), k) * jnp.exp(d - m_t)
        inter = jnp.exp(g_inter - m_t)
        den = (jnp.sum(sw, axis=-1, keepdims=True)
               + inter * jnp.sum(jnp.where(hm, q_n, 0.0), axis=-1, keepdims=True))
        inv = 1.0 / jnp.maximum(jnp.abs(den), jnp.exp(-m_t))
        hh = jnp.where(hm, (_mm(sw, v) + inter * q_c) * inv, hh)

        g_end_r = f_last - fr + igr
        m_new = jnp.maximum(f_last + m0, jnp.max(g_end_r, axis=-1, keepdims=True))
        carry_decay = jnp.exp(f_last + m0 - m_new)
        w_all = jnp.where(hm, jnp.exp(f_last - fc + igc - m_new), w_all)
        cd_lane = jnp.where(lane_head_row == h, carry_decay, cd_lane)
        cd_rows = jnp.where(row_head_col == h, carry_decay, cd_rows)
        m_ref[h] = m_new

    sq = hh * hh
    ms = jnp.zeros((L, W), F32)
    for h in range(B_HEADS):
        hm = lane_head == h
        ms = jnp.where(hm, jnp.sum(jnp.where(hm, sq, 0.0), axis=-1, keepdims=True) * inv_d, ms)
    o_ref[...] = hh * lax.rsqrt(ms + EPS) * hng_ref[...] * jax.nn.sigmoid(o)

    c_ref[...] = cd_rows * c0 + jnp.where(same_head, _mm_tn(v * w_all, k), 0.0)
    n_ref[...] = cd_lane * n0 + jnp.sum(k * w_all, axis=0, keepdims=True)


def _mlstm(p_b, p_if, g_t, conv_w, conv_b, bif_row, bif_col, hn_g, ltri, utri, seq):
    nt = p_b.shape[0]
    L = B_CHUNK
    nc = seq // L
    return pl.pallas_call(
        functools.partial(_mlstm_kernel, chunks_per_seq=nc),
        grid=(nt // L,),
        in_specs=[pl.BlockSpec((L, 4 * B_WIDTH), lambda i: (i, 0)),
                  pl.BlockSpec((SUBLANES, 4 * B_WIDTH), lambda i: (jnp.maximum(i * (L // SUBLANES) - 1, 0), 0)),
                  pl.BlockSpec((L, LANES), lambda i: (i, 0)),
                  pl.BlockSpec((1, SUBLANES, L), lambda i: (i // nc, 0, i % nc)),
                  _full(conv_w.shape), _full(conv_b.shape), _full(bif_row.shape), _full(bif_col.shape),
                  _full(hn_g.shape), _full(ltri.shape), _full(utri.shape)],
        out_specs=pl.BlockSpec((L, B_WIDTH), lambda i: (i, 0)),
        out_shape=jax.ShapeDtypeStruct((nt, B_WIDTH), F32),
        scratch_shapes=[pltpu.VMEM((B_WIDTH, B_WIDTH), F32),
                        pltpu.VMEM((1, B_WIDTH), F32),
                        pltpu.VMEM((B_HEADS, 1, 1), F32)],
        compiler_params=_params(),
        name="mlstm",
    )(p_b, p_b, p_if, g_t, conv_w, conv_b, bif_row, bif_col, hn_g, ltri, utri)


def _swa_kernel(q_ref, kvc_ref, kvp_ref, qn_ref, kn_ref, sink_ref, o_ref, *, blocks_per_seq):
    i = pl.program_id(0)
    first = (i % blocks_per_seq) == 0
    bk = C_BLOCK
    hp = LANES
    group = C_Q_HEADS // C_KV_HEADS
    rows = group * bk
    kv = jnp.concatenate([kvp_ref[...], kvc_ref[...]], axis=0)
    qrow = lax.broadcasted_iota(jnp.int32, (rows, 2 * bk), 0)
    kj = lax.broadcasted_iota(jnp.int32, (rows, 2 * bk), 1)
    dist = qrow % bk + bk - kj
    valid = (dist >= 0) & (dist < WINDOW) & (kj >= jnp.where(first, bk, 0))
    distf = dist.astype(F32)
    rgroup = lax.broadcasted_iota(jnp.int32, (rows, 1), 0) // bk
    inv_d = 1.0 / C_HEAD_DIM
    qn = qn_ref[...]
    kn = kn_ref[...]

    def per_row_group(vals):
        out = vals[-1]
        for j in range(group - 2, -1, -1):
            out = jnp.where(rgroup == j, vals[j], out)
        return out

    for kvh in range(C_KV_HEADS):
        kh = kv[:, kvh * hp:(kvh + 1) * hp]
        kh = kh * lax.rsqrt(jnp.sum(kh * kh, axis=-1, keepdims=True) * inv_d + EPS) * kn
        vh = kv[:, (C_KV_HEADS + kvh) * hp:(C_KV_HEADS + kvh + 1) * hp]
        heads = [kvh * group + j for j in range(group)]
        q4 = jnp.concatenate([q_ref[:, h * hp:(h + 1) * hp] for h in heads], axis=0)
        q4 = q4 * lax.rsqrt(jnp.sum(q4 * q4, axis=-1, keepdims=True) * inv_d + EPS) * qn * (C_HEAD_DIM ** -0.5)
        slope = per_row_group([float(2.0 ** (-8.0 * (h + 1) / C_Q_HEADS)) for h in heads])
        sink = per_row_group([sink_ref[:, h:h + 1] for h in heads])
        s = jnp.where(valid, _mm_nt(q4, kh) - slope * distf, -jnp.inf)
        m = jnp.maximum(jnp.max(s, axis=-1, keepdims=True), sink)
        p = jnp.exp(s - m)
        den = jnp.sum(p, axis=-1, keepdims=True) + jnp.exp(sink - m)
        out = _mm(p, vh) * (1.0 / den)
        for pair in range(group // 2):
            even = out[(2 * pair) * bk:(2 * pair + 1) * bk]
            odd = out[(2 * pair + 1) * bk:(2 * pair + 2) * bk]
            t = kvh * (group // 2) + pair
            o_ref[:, t * hp:(t + 1) * hp] = even + pltpu.roll(odd, C_HEAD_DIM, 1)


def _swa(p_cq, p_ckv, qn_g, kn_g, sink_row, seq):
    nt = p_cq.shape[0]
    bk = C_BLOCK
    nb = seq // bk
    return pl.pallas_call(
        functools.partial(_swa_kernel, blocks_per_seq=nb),
        grid=(nt // bk,),
        in_specs=[pl.BlockSpec((bk, p_cq.shape[1]), lambda i: (i, 0)),
                  pl.BlockSpec((bk, p_ckv.shape[1]), lambda i: (i, 0)),
                  pl.BlockSpec((bk, p_ckv.shape[1]), lambda i: (jnp.maximum(i - 1, 0), 0)),
                  _full(qn_g.shape), _full(kn_g.shape), _full(sink_row.shape)],
        out_specs=pl.BlockSpec((bk, C_WIDTH), lambda i: (i, 0)),
        out_shape=jax.ShapeDtypeStruct((nt, C_WIDTH), F32),
        compiler_params=_params(),
        name="swa",
    )(p_cq, p_ckv, p_ckv, qn_g, kn_g, sink_row)


def _merge_kernel(x_ref, gt_ref, pg_ref, ya_ref, r_ref, k_ref, v_ref, g_ref, yb_ref, yc_ref,
                  hsum_ref, lng_ref, lnb_ref, rk_ref, wa_ref, wb_ref, wc_ref, wo_ref, o_ref):
    hsum = hsum_ref[...]
    inv_n = 1.0 / A_HEAD_DIM
    y = ya_ref[...]
    mu = _mm32(y, hsum) * inv_n
    dlt = y - mu
    var = _mm32(dlt * dlt, hsum) * inv_n
    yn = dlt * lax.rsqrt(var + A_LN_EPS) * lng_ref[...] + lnb_ref[...]
    bonus = _mm32(r_ref[...] * k_ref[...] * rk_ref[...], hsum) * v_ref[...]
    y_a = (yn + bonus) * g_ref[...]
    d = x_ref.shape[1]
    pg = pg_ref[...]
    merged = (jax.nn.sigmoid(pg[:, :d]) * _mm(y_a, wa_ref[...])
              + jax.nn.sigmoid(pg[:, d:2 * d]) * _mm(yb_ref[...], wb_ref[...])
              + jax.nn.sigmoid(pg[:, 2 * d:]) * _mm(yc_ref[...], wc_ref[...]))
    o_ref[...] = x_ref[...] + gt_ref[0] * _mm(merged, wo_ref[...])


def _merge(x2, gt, p_g, y_scan, r, k, v, g, y_b, y_c, hsum, lng, lnb, rk, wa, wb, wc, wo, seq):
    nt, d = x2.shape
    tm = ROW_TILE
    tps = seq // tm
    row = lambda w: pl.BlockSpec((tm, w), lambda i: (i, 0))
    consts = (hsum, lng, lnb, rk, wa, wb, wc, wo)
    return pl.pallas_call(
        _merge_kernel,
        grid=(nt // tm,),
        in_specs=[row(d), pl.BlockSpec((1, 1, d), lambda i: (i // tps, 0, 0)), row(3 * d)]
                 + [row(A_WIDTH)] * 5 + [row(B_WIDTH), row(C_WIDTH)] + [_full(c.shape) for c in consts],
        out_specs=row(d),
        out_shape=jax.ShapeDtypeStruct((nt, d), F32),
        compiler_params=_params(),
        name="merge",
    )(x2, gt, p_g, y_scan, r, k, v, g, y_b, y_c, *consts)


def _route_kernel(x_ref, g_ref, sc_ref, sh_ref, wr_ref, br_ref, lstrict_ref, h_ref, idx_ref, wgt_ref, cnt_ref):
    x = x_ref[...]
    y = x * lax.rsqrt(jnp.mean(x * x, axis=-1, keepdims=True) + EPS) * g_ref[...]
    h = y * (1.0 + sc_ref[0]) + sh_ref[0]
    h_ref[...] = h
    lg = _mm32(h, wr_ref[...]) + br_ref[...]
    lane = lax.broadcasted_iota(jnp.int32, lg.shape, 1)
    lanef = lane.astype(F32)
    ninf = -jnp.inf
    big = float(LANES)

    def first_argmax(vals, vmax):
        return jnp.min(jnp.where(vals == vmax, lanef, big), axis=-1, keepdims=True)

    gl = jnp.where(lane < N_GROUPS, lg, ninf)
    gmax = jnp.max(gl, axis=-1, keepdims=True)
    g_top = 1.0 / jnp.sum(jnp.exp(gl - gmax), axis=-1, keepdims=True)
    g_idx = first_argmax(gl, gmax)
    lo = N_GROUPS + EXPERTS_PER_GROUP * g_idx
    el = jnp.where((lanef >= lo) & (lanef < lo + EXPERTS_PER_GROUP), lg, ninf)
    e1 = jnp.max(el, axis=-1, keepdims=True)
    z = jnp.sum(jnp.exp(el - e1), axis=-1, keepdims=True)
    i1 = first_argmax(el, e1)
    el2 = jnp.where(lanef == i1, ninf, el)
    e2 = jnp.max(el2, axis=-1, keepdims=True)
    i2 = first_argmax(el2, e2)
    p1 = 1.0 / z
    p2 = jnp.exp(e2 - e1) / z
    w1 = g_top * p1 / (p1 + p2)
    w2 = g_top * p2 / (p1 + p2)
    oh1 = jnp.where(lanef == i1, 1.0, 0.0)
    oh2 = jnp.where(lanef == i2, 1.0, 0.0)
    both = oh1 + oh2
    before = _mm32(lstrict_ref[...], both)
    r1 = jnp.sum(before * oh1, axis=-1, keepdims=True)
    r2 = jnp.sum(before * oh2, axis=-1, keepdims=True)
    idx = jnp.where(lane == 0, i1 - N_GROUPS, jnp.where(lane == 1, i2 - N_GROUPS,
                    jnp.where(lane == 2, r1, jnp.where(lane == 3, r2, 0.0))))
    idx_ref[...] = idx.astype(jnp.int32)
    wgt_ref[...] = jnp.where(lane == 0, w1, jnp.where(lane == 1, w2, 0.0))
    counts = jnp.sum(both, axis=0, keepdims=True)
    cnt_ref[0] = jnp.broadcast_to(counts, cnt_ref.shape[1:]).astype(jnp.int32)


def _route(x2, g, sc, sh, w_r, b_r, seq):
    nt, d = x2.shape
    tm = ROW_TILE
    tps = seq // tm
    n_tiles = nt // tm
    row = lambda w: pl.BlockSpec((tm, w), lambda i: (i, 0))
    mod = pl.BlockSpec((1, 1, d), lambda i: (i // tps, 0, 0))
    lstrict = jnp.tril(jnp.ones((tm, tm), F32), k=-1)
    return pl.pallas_call(
        _route_kernel,
        grid=(n_tiles,),
        in_specs=[row(d), _full((1, d)), mod, mod, _full(w_r.shape), _full(b_r.shape), _full(lstrict.shape)],
        out_specs=[row(d), row(LANES), row(LANES), pl.BlockSpec((1, SUBLANES, LANES), lambda i: (i, 0, 0))],
        out_shape=[jax.ShapeDtypeStruct((nt, d), F32), jax.ShapeDtypeStruct((nt, LANES), jnp.int32),
                   jax.ShapeDtypeStruct((nt, LANES), F32),
                   jax.ShapeDtypeStruct((n_tiles, SUBLANES, LANES), jnp.int32)],
        compiler_params=_params(),
        name="route",
    )(x2, g, sc, sh, w_r, b_r, lstrict)


def _expert_kernel(bexp_ref, tok_ref, tok_next_ref, h_hbm, win_ref, wout_ref, o_ref, buf, sem):
    del bexp_ref
    slot = _pipelined_row_gather(tok_ref, tok_next_ref, MOE_BLOCK, h_hbm, buf, sem)
    gu = _mm(buf[slot], win_ref[0])
    gate = gu[:, :D_EXPERT]
    up = gu[:, D_EXPERT:]
    o_ref[...] = _mm(gate * jax.nn.sigmoid(gate) * up, wout_ref[0])


def _experts(block_exp, slot_tok, h2, w_in_e, w_out_e):
    n_blocks = block_exp.shape[0]
    rows = MOE_BLOCK
    d = h2.shape[1]
    tok3 = slot_tok.reshape(n_blocks, 1, rows)
    smem_blk = lambda f: pl.BlockSpec((1, 1, rows), f, memory_space=pltpu.SMEM)
    grid_spec = pltpu.PrefetchScalarGridSpec(
        num_scalar_prefetch=1,
        grid=(n_blocks,),
        in_specs=[smem_blk(lambda i, be: (i, 0, 0)),
                  smem_blk(lambda i, be: (jnp.minimum(i + 1, n_blocks - 1), 0, 0)),
                  pl.BlockSpec(memory_space=pl.ANY),
                  pl.BlockSpec((1, d, 2 * D_EXPERT), lambda i, be: (be[i], 0, 0)),
                  pl.BlockSpec((1, D_EXPERT, d), lambda i, be: (be[i], 0, 0))],
        out_specs=pl.BlockSpec((rows, d), lambda i, be: (i, 0)),
        scratch_shapes=[pltpu.VMEM((2, rows, d), F32), pltpu.SemaphoreType.DMA((2,))],
    )
    return pl.pallas_call(
        _expert_kernel,
        grid_spec=grid_spec,
        out_shape=jax.ShapeDtypeStruct((n_blocks * rows, d), F32),
        compiler_params=_params(),
        name="experts",
    )(block_exp, tok3, tok3, h2, w_in_e, w_out_e)


def _combine_kernel(pos_ref, pos_next_ref, x_ref, gt_ref, wgt_ref, yb_hbm, o_ref, buf, sem):
    o_ref[...] = _moe_residual(pos_ref, pos_next_ref, x_ref, gt_ref, wgt_ref, yb_hbm, buf, sem)


def _combine(pos, x2, gt, wgt, yb, seq):
    nt, d = x2.shape
    tm = ROW_TILE
    n_tiles = nt // tm
    pos3, specs, scratch = _moe_residual_specs(pos, tm, n_tiles, d, seq // tm)
    return pl.pallas_call(
        _combine_kernel,
        grid=(n_tiles,),
        in_specs=specs,
        out_specs=pl.BlockSpec((tm, d), lambda i: (i, 0)),
        out_shape=jax.ShapeDtypeStruct((nt, d), F32),
        scratch_shapes=scratch,
        compiler_params=_params(),
        name="combine",
    )(pos3, pos3, x2, gt, wgt, yb)


def _dispatch_plan(ridx, tile_counts, n_tokens):
    n_tiles = tile_counts.shape[0]
    tm = n_tokens // n_tiles
    na = n_tokens * TOP_K
    counts = tile_counts[:, 0, N_GROUPS:N_GROUPS + N_EXPERTS]
    tile_off = jnp.cumsum(counts, axis=0) - counts
    total = jnp.sum(counts, axis=0)
    padded = (total + MOE_BLOCK - 1) // MOE_BLOCK * MOE_BLOCK
    pad_ends = jnp.cumsum(padded)
    base = (pad_ends - padded)[None, :] + tile_off
    expert = ridx[:, :TOP_K].reshape(n_tiles, tm, TOP_K)
    rank = ridx[:, TOP_K:2 * TOP_K].reshape(n_tiles, tm, TOP_K)
    onehot = expert[..., None] == jnp.arange(N_EXPERTS, dtype=jnp.int32)
    dest = jnp.sum(jnp.where(onehot, base[:, None, None, :], 0), axis=-1) + rank
    dest = dest.reshape(n_tokens, TOP_K)
    n_blocks = -(-na // MOE_BLOCK) + N_EXPERTS
    slot_tok = jnp.zeros((n_blocks * MOE_BLOCK,), jnp.int32).at[dest.reshape(-1)].set(
        jnp.arange(na, dtype=jnp.int32) // TOP_K)
    block_exp = jnp.minimum(
        jnp.searchsorted(pad_ends, jnp.arange(n_blocks, dtype=jnp.int32) * MOE_BLOCK, side='right'),
        N_EXPERTS - 1).astype(jnp.int32)
    return block_exp, slot_tok, dest


def _pad_cols(w, width):
    return jnp.pad(w, ((0, 0), (0, width - w.shape[1])))


def _pad_heads(w, head_dim):
    rows = w.shape[0]
    w = w.reshape(rows, -1, head_dim)
    return jnp.pad(w, ((0, 0), (0, 0), (0, LANES - head_dim))).reshape(rows, -1)


def kernel(x, c, w_ada, b_ada, norm1_g, norm2_g, w_in, mu_a, w0_a, w2_a, a0_a, a2_a, g2_a, kk_a, ka_a, rk_a,
           lnx_g_a, lnx_b_a, conv_w_b, conv_b_b, bi_b, bf_b, hn_g_b, qn_g_c, kn_g_c, sink_c, w_br_a, w_br_b,
           w_br_c, w_out, w_rg, b_rg, w_re, b_re, w_e_in, w_e_out):
    bsz, seq, d = x.shape
    nt = bsz * seq
    depth = w_in.shape[0]
    assert bsz * A_HEADS * 4 == LANES, "the RWKV recurrence packs (batch, head, 4) into the lane axis"
    assert seq % ROW_TILE == 0 and seq % B_CHUNK == 0 and seq % SCAN_STEPS == 0

    mod = _ada_mod(c, w_ada, b_ada)
    hsum = jnp.kron(jnp.eye(A_HEADS, dtype=F32), jnp.ones((A_HEAD_DIM, A_HEAD_DIM), F32))
    tri = jnp.tril(jnp.ones((B_CHUNK, B_CHUNK), F32))

    o_b = 1024
    o_if = o_b + 4 * B_WIDTH
    o_cq = o_if + 2 * B_HEADS
    o_ckv = o_cq + C_WIDTH
    o_g = o_ckv + 2 * C_KV_WIDTH

    x2 = x.reshape(nt, d)
    moe = None
    for l in range(depth):
        m = mod[l].reshape(bsz, 6, 1, d)
        sh1, sc1, gt1, sh2, sc2, gt2 = (m[:, i] for i in range(6))
        wl = w_in[l]
        weights = [wl[:, o_g:].astype(BF16), wl[:, :o_b].astype(BF16), wl[:, o_b:o_if].astype(BF16),
                   _pad_heads(wl[:, o_cq:o_ckv], C_HEAD_DIM).astype(BF16),
                   _pad_heads(wl[:, o_ckv:o_g], C_HEAD_DIM).astype(BF16),
                   _pad_cols(wl[:, o_if:o_cq], LANES).astype(BF16)]
        outs = _norm_proj(x2, norm1_g[l][None], sc1, sh1, weights, seq, moe)
        if moe is not None:
            x2 = outs.pop(0)
        p_g, p_a, p_b, p_cq, p_ckv, p_if = outs

        wwa = (jnp.zeros((A_DECAY_LORA + A_ICL_LORA, 2 * A_WIDTH), F32)
               .at[:A_DECAY_LORA, :A_WIDTH].set(w2_a[l]).at[A_DECAY_LORA:, A_WIDTH:].set(a2_a[l]))
        r, dec, k2, v, kk, bvec, g = _rwkv_prep(p_a, mu_a[l][None], wwa, w0_a[l][None], a0_a[l][None], g2_a[l],
                                                kk_a[l][None], ka_a[l][None], hsum, seq)
        y_scan = _rwkv_scan(*(_expand_key(t, bsz, seq) for t in (dec, kk, bvec, k2, r)),
                            _permute_value(v, bsz, seq))
        y_scan = _unpermute_value(y_scan, bsz, seq)

        g_t = p_if[:, :SUBLANES].reshape(bsz, seq, SUBLANES).transpose(0, 2, 1)
        bif = jnp.concatenate([bi_b[l], bf_b[l]])
        y_b = _mlstm(p_b, p_if, g_t, conv_w_b[l], conv_b_b[l][None], _pad_cols(bif[None], LANES), bif[:, None],
                     hn_g_b[l].reshape(1, B_WIDTH), tri, tri.T, seq)

        y_c = _swa(p_cq, p_ckv, _pad_cols(qn_g_c[l][None], LANES), _pad_cols(kn_g_c[l][None], LANES),
                   _pad_cols(sink_c[l][None], LANES), seq)

        x2 = _merge(x2, gt1, p_g, y_scan, r, k2, v, g, y_b, y_c, hsum, lnx_g_a[l][None], lnx_b_a[l][None],
                    rk_a[l].reshape(1, A_WIDTH), w_br_a[l].astype(BF16), w_br_b[l].astype(BF16),
                    w_br_c[l].astype(BF16), w_out[l].astype(BF16), seq)

        w_r = _pad_cols(jnp.concatenate([w_rg[l], w_re[l]], axis=1), LANES)
        b_r = _pad_cols(jnp.concatenate([b_rg[l], b_re[l]])[None], LANES)
        h2, ridx, rwgt, tile_counts = _route(x2, norm2_g[l][None], sc2, sh2, w_r, b_r, seq)
        block_exp, slot_tok, pos = _dispatch_plan(ridx, tile_counts, nt)
        yb = _experts(block_exp, slot_tok, h2, w_e_in[l].astype(BF16), w_e_out[l].astype(BF16))
        moe = (pos, gt2, rwgt, yb)
    pos, gt2, rwgt, yb = moe
    x2 = _combine(pos, x2, gt2, rwgt, yb, seq)
    return x2.reshape(bsz, seq, d)
```

```python
import functools

import numpy as np
import jax
import jax.numpy as jnp
from jax import lax
from jax.experimental import pallas as pl
from jax.experimental.pallas import tpu as pltpu

F32 = jnp.float32
BF16 = jnp.bfloat16
HIGHEST = lax.Precision.HIGHEST

A_HEADS = 4
A_HEAD_DIM = 64
A_WIDTH = 256
A_DECAY_LORA = 64
A_ICL_LORA = 64
A_LN_EPS = 64e-5
B_HEADS = 4
B_HEAD_DIM = 64
B_WIDTH = 256
B_CHUNK = 128
C_Q_HEADS = 8
C_KV_HEADS = 2
C_HEAD_DIM = 64
C_WIDTH = 512
C_KV_WIDTH = 128
WINDOW = 128
C_BLOCK = 128
N_GROUPS = 4
EXPERTS_PER_GROUP = 8
N_EXPERTS = 32
TOP_K = 2
D_EXPERT = 512
MOE_BLOCK = 128
EPS = 1e-6

LANES = 128
SUBLANES = 8
VMEM_LIMIT = 56 * 1024 * 1024

ROW_TILE = 256
SCAN_STEPS = 32
ADA_COL_TILE = 1536


def _mm(a, b):
    return jnp.dot(a.astype(BF16), b.astype(BF16), preferred_element_type=F32)


def _mm32(a, b):
    return jnp.dot(a, b, precision=HIGHEST, preferred_element_type=F32)


def _mm_nt(a, b):
    return lax.dot_general(a.astype(BF16), b.astype(BF16), (((1,), (1,)), ((), ())),
                           preferred_element_type=F32)


def _mm_tn(a, b):
    return lax.dot_general(a.astype(BF16), b.astype(BF16), (((0,), (0,)), ((), ())),
                           preferred_element_type=F32)


def _log_sigmoid(x):
    return jnp.minimum(x, 0.0) - jnp.log(1.0 + jnp.exp(-jnp.abs(x)))


def _params(n_axes=1):
    return pltpu.CompilerParams(dimension_semantics=("arbitrary",) * n_axes,
                                vmem_limit_bytes=VMEM_LIMIT)


def _full(shape):
    nd = len(shape)
    return pl.BlockSpec(shape, lambda *_: (0,) * nd)


def _ada_kernel(c_ref, w_ref, b_ref, o_ref):
    c = c_ref[...]
    cond = c * jax.nn.sigmoid(c)
    o_ref[0] = _mm(cond, w_ref[0]) + b_ref[0]


def _ada_mod(c, w_ada, b_ada):
    depth, d, n = w_ada.shape
    bsz = c.shape[0]
    tn = ADA_COL_TILE
    return pl.pallas_call(
        _ada_kernel,
        grid=(depth, n // tn),
        in_specs=[pl.BlockSpec((bsz, d), lambda l, j: (0, 0)),
                  pl.BlockSpec((1, d, tn), lambda l, j: (l, 0, j)),
                  pl.BlockSpec((1, 1, tn), lambda l, j: (l, 0, j))],
        out_specs=pl.BlockSpec((1, bsz, tn), lambda l, j: (l, 0, j)),
        out_shape=jax.ShapeDtypeStruct((depth, bsz, n), F32),
        compiler_params=_params(2),
        name="ada_mod",
    )(c, w_ada, b_ada.reshape(depth, 1, n))


def _start_row_gather(idx_ref, n_rows, src_hbm, dst, sem):
    for r in range(n_rows):
        pltpu.make_async_copy(src_hbm.at[pl.ds(idx_ref[0, 0, r], 1), :], dst.at[pl.ds(r, 1), :],
                              sem).start(priority=r % 2)


def _wait_row_gather(n_rows, src_hbm, dst, sem):
    pltpu.make_async_copy(src_hbm.at[pl.ds(0, n_rows), :], dst, sem).wait()


def _pipelined_row_gather(idx_ref, idx_next_ref, n_rows, src_hbm, buf, sem):
    i = pl.program_id(0)
    slot = i % 2

    @pl.when(i == 0)
    def _():
        _start_row_gather(idx_ref, n_rows, src_hbm, buf.at[0], sem.at[0])

    _start_row_gather(idx_next_ref, n_rows, src_hbm, buf.at[1 - slot], sem.at[1 - slot])
    _wait_row_gather(n_rows, src_hbm, buf.at[slot], sem.at[slot])

    @pl.when(i == pl.num_programs(0) - 1)
    def _():
        _wait_row_gather(n_rows, src_hbm, buf.at[1 - slot], sem.at[1 - slot])

    return slot


def _moe_residual(pos_ref, pos_next_ref, x_ref, gt_ref, wgt_ref, yb_hbm, buf, sem):
    tm = x_ref.shape[0]
    slot = _pipelined_row_gather(pos_ref, pos_next_ref, TOP_K * tm, yb_hbm, buf, sem)
    wgt = wgt_ref[...]
    y = buf[slot, :tm, :] * wgt[:, 0:1] + buf[slot, tm:, :] * wgt[:, 1:2]
    return x_ref[...] + gt_ref[0] * y


def _moe_residual_specs(pos, tm, n_tiles, d, tps):
    pos3 = pos.reshape(n_tiles, tm, TOP_K).transpose(0, 2, 1).reshape(n_tiles, 1, TOP_K * tm)
    smem_blk = lambda f: pl.BlockSpec((1, 1, TOP_K * tm), f, memory_space=pltpu.SMEM)
    row = lambda w: pl.BlockSpec((tm, w), lambda i: (i, 0))
    specs = [smem_blk(lambda i: (i, 0, 0)),
             smem_blk(lambda i: (jnp.minimum(i + 1, n_tiles - 1), 0, 0)),
             row(d), pl.BlockSpec((1, 1, d), lambda i: (i // tps, 0, 0)), row(LANES),
             pl.BlockSpec(memory_space=pl.ANY)]
    scratch = [pltpu.VMEM((2, TOP_K * tm, d), F32), pltpu.SemaphoreType.DMA((2,))]
    return pos3, specs, scratch


def _norm_proj_kernel(*refs, n_proj, moe):
    if moe:
        x = _moe_residual(*refs[:6], *refs[-2:])
        refs = refs[6:-2]
        refs[-1][...] = x
        refs = refs[:-1]
    else:
        x = refs[0][...]
        refs = refs[1:]
    g_ref, sc_ref, sh_ref = refs[:3]
    y = x * lax.rsqrt(jnp.mean(x * x, axis=-1, keepdims=True) + EPS) * g_ref[...]
    h = (y * (1.0 + sc_ref[0]) + sh_ref[0]).astype(BF16)
    for w_ref, o_ref in zip(refs[3:3 + n_proj], refs[3 + n_proj:]):
        o_ref[...] = jnp.dot(h, w_ref[...], preferred_element_type=F32)


def _norm_proj(x2, g, sc, sh, weights, seq, moe=None):
    nt, d = x2.shape
    tm = ROW_TILE
    tps = seq // tm
    n_tiles = nt // tm
    row = lambda w: pl.BlockSpec((tm, w), lambda i: (i, 0))
    mod = pl.BlockSpec((1, 1, d), lambda i: (i // tps, 0, 0))
    in_specs = [_full((1, d)), mod, mod] + [_full(w.shape) for w in weights]
    out_specs = [row(w.shape[1]) for w in weights]
    out_shape = [jax.ShapeDtypeStruct((nt, w.shape[1]), F32) for w in weights]
    if moe is None:
        args, head_specs, scratch = [x2], [row(d)], []
    else:
        pos, gt, wgt, yb = moe
        pos3, head_specs, scratch = _moe_residual_specs(pos, tm, n_tiles, d, tps)
        args = [pos3, pos3, x2, gt, wgt, yb]
        out_specs.append(row(d))
        out_shape.append(jax.ShapeDtypeStruct((nt, d), F32))
    outs = pl.pallas_call(
        functools.partial(_norm_proj_kernel, n_proj=len(weights), moe=moe is not None),
        grid=(n_tiles,),
        in_specs=head_specs + in_specs,
        out_specs=out_specs,
        out_shape=out_shape,
        scratch_shapes=scratch,
        compiler_params=_params(),
        name="norm_proj_moe" if moe is not None else "norm_proj",
    )(*args, g, sc, sh, *weights)
    outs = list(outs)
    return outs if moe is None else [outs[-1]] + outs[:-1]


def _rwkv_prep_kernel(p_ref, prev_ref, mu_ref, wwa_ref, w0_ref, a0_ref, g2_ref, kkw_ref, kaw_ref, hsum_ref,
                      r_out, w_out, k_out, v_out, kk_out, b_out, g_out, *, tiles_per_seq):
    i = pl.program_id(0)
    p = p_ref[...]
    first = (i % tiles_per_seq) == 0
    prev_row = jnp.where(first, 0.0, prev_ref[SUBLANES - 1:SUBLANES, :])
    row = lax.broadcasted_iota(jnp.int32, p.shape, 0)
    shifted = jnp.where(row == 0, prev_row, pltpu.roll(p, 1, 0))
    pa = p + mu_ref[...] * (shifted - p)
    aw = A_WIDTH
    o_lora = 3 * aw
    o_glo = o_lora + A_DECAY_LORA + A_ICL_LORA
    r = pa[:, 0:aw]
    k = pa[:, aw:2 * aw]
    v = pa[:, 2 * aw:o_lora]
    slab = pa[:, o_lora:o_glo]
    glo = pa[:, o_glo:]
    lane = lax.broadcasted_iota(jnp.int32, slab.shape, 1)
    z = jnp.where(lane < A_DECAY_LORA, jnp.tanh(slab), slab)
    wa = _mm32(z, wwa_ref[...])
    w_log = _log_sigmoid(w0_ref[...] + wa[:, :aw]) - 0.5
    decay = jnp.exp(-jnp.exp(w_log))
    a = jax.nn.sigmoid(a0_ref[...] + wa[:, aw:])
    g = _mm32(jax.nn.sigmoid(glo), g2_ref[...])
    kk = k * kkw_ref[...]
    ss = _mm32(kk * kk, hsum_ref[...])
    kk = kk / jnp.maximum(jnp.sqrt(ss), 1e-12)
    r_out[...] = r
    w_out[...] = decay
    k_out[...] = k * (1.0 + (a - 1.0) * kaw_ref[...])
    v_out[...] = v
    kk_out[...] = kk
    b_out[...] = kk * a
    g_out[...] = g


def _rwkv_prep(p_a, mu, wwa, w0, a0, g2, kkw, kaw, hsum, seq):
    nt, wd = p_a.shape
    tm = min(ROW_TILE, seq)
    tps = seq // tm
    row = lambda w: pl.BlockSpec((tm, w), lambda i: (i, 0))
    prev = pl.BlockSpec((SUBLANES, wd), lambda i: (jnp.maximum(i * (tm // SUBLANES) - 1, 0), 0))
    outs = [jax.ShapeDtypeStruct((nt, A_WIDTH), F32)] * 7
    return pl.pallas_call(
        functools.partial(_rwkv_prep_kernel, tiles_per_seq=tps),
        grid=(nt // tm,),
        in_specs=[row(wd), prev, _full(mu.shape), _full(wwa.shape), _full(w0.shape), _full(a0.shape),
                  _full(g2.shape), _full(kkw.shape), _full(kaw.shape), _full(hsum.shape)],
        out_specs=[row(A_WIDTH)] * 7,
        out_shape=outs,
        compiler_params=_params(),
        name="rwkv_prep",
    )(p_a, p_a, mu, wwa, w0, a0, g2, kkw, kaw, hsum)


VAL_LANE_REPS = 4
VAL_TILES = A_HEAD_DIM // (SUBLANES * VAL_LANE_REPS)
SUM_CHAINS = 4


def _rwkv_scan_kernel(w_ref, kk_ref, b_ref, k_ref, r_ref, v_ref, y_ref, s_ref, *, steps):
    @pl.when(pl.program_id(0) == 0)
    def _():
        s_ref[...] = jnp.zeros_like(s_ref)

    def tree(terms):
        while len(terms) > 1:
            terms = [terms[i] + terms[i + 1] for i in range(0, len(terms), 2)]
        return terms[0]

    def step(t, carry):
        def row(ref, j):
            return jnp.broadcast_to(ref[t, j:j + 1, :], (SUBLANES, LANES))

        tiles = range(VAL_TILES)
        v_t = [v_ref[t, ih * SUBLANES:(ih + 1) * SUBLANES, :] for ih in tiles]
        acc = [[None] * SUM_CHAINS for _ in tiles]
        for j in range(A_HEAD_DIM):
            kk = row(kk_ref, j)
            for ih in tiles:
                term = s_ref[ih * A_HEAD_DIM + j] * kk
                c = j % SUM_CHAINS
                acc[ih][c] = term if acc[ih][c] is None else acc[ih][c] + term
        sa = [tree(a) for a in acc]
        y = [[None] * SUM_CHAINS for _ in tiles]
        for j in range(A_HEAD_DIM):
            w, b, k, r = (row(ref, j) for ref in (w_ref, b_ref, k_ref, r_ref))
            for ih in tiles:
                idx = ih * A_HEAD_DIM + j
                s = s_ref[idx] * w - sa[ih] * b + v_t[ih] * k
                s_ref[idx] = s
                term = s * r
                c = j % SUM_CHAINS
                y[ih][c] = term if y[ih][c] is None else y[ih][c] + term
        for ih in tiles:
            y_ref[t, ih * SUBLANES:(ih + 1) * SUBLANES, :] = tree(y[ih])
        return carry

    lax.fori_loop(0, steps, step, 0)


def _rwkv_scan(w_e, kk_e, b_e, k_e, r_e, v_p):
    seq = w_e.shape[0]
    tc = SCAN_STEPS
    key = pl.BlockSpec((tc, A_HEAD_DIM, LANES), lambda i: (i, 0, 0))
    val = pl.BlockSpec((tc, VAL_TILES * SUBLANES, LANES), lambda i: (i, 0, 0))
    return pl.pallas_call(
        functools.partial(_rwkv_scan_kernel, steps=tc),
        grid=(seq // tc,),
        in_specs=[key] * 5 + [val],
        out_specs=val,
        out_shape=jax.ShapeDtypeStruct((seq, VAL_TILES * SUBLANES, LANES), F32),
        scratch_shapes=[pltpu.VMEM((VAL_TILES * A_HEAD_DIM, SUBLANES, LANES), F32)],
        compiler_params=_params(),
        name="rwkv_scan",
    )(w_e, kk_e, b_e, k_e, r_e, v_p)


def _key_layout(x, bsz, seq):
    x = x.reshape(bsz, seq, A_HEADS, A_HEAD_DIM).transpose(1, 3, 0, 2)
    x = jnp.broadcast_to(x[..., None], (seq, A_HEAD_DIM, bsz, A_HEADS, VAL_LANE_REPS))
    return x.reshape(seq, A_HEAD_DIM, LANES)


def _value_layout(x, bsz, seq):
    x = x.reshape(bsz, seq, A_HEADS, VAL_TILES, SUBLANES, VAL_LANE_REPS).transpose(1, 3, 4, 0, 2, 5)
    return x.reshape(seq, VAL_TILES * SUBLANES, LANES)


def _value_unlayout(y, bsz, seq):
    y = y.reshape(seq, VAL_TILES, SUBLANES, bsz, A_HEADS, VAL_LANE_REPS).transpose(3, 0, 4, 1, 2, 5)
    return y.reshape(bsz * seq, A_WIDTH)


def _mlstm_kernel(pb_ref, prev_ref, pif_ref, gt_ref, cw_ref, cb_ref, bifr_ref, bifc_ref, hng_ref,
                  ltri_ref, utri_ref, o_ref, c_ref, n_ref, m_ref, *, chunks_per_seq):
    i = pl.program_id(0)
    first = (i % chunks_per_seq) == 0

    @pl.when(first)
    def _():
        c_ref[...] = jnp.zeros_like(c_ref)
        n_ref[...] = jnp.zeros_like(n_ref)
        m_ref[...] = jnp.zeros_like(m_ref)

    L = B_CHUNK
    pb = pb_ref[...]
    x = pb[:, :2 * B_WIDTH]
    prev = jnp.where(first, 0.0, prev_ref[:, :2 * B_WIDTH])
    cw = cw_ref[...]
    taps = cw.shape[0]
    acc = x * cw[taps - 1:taps, :] + cb_ref[...]
    r8 = lax.broadcasted_iota(jnp.int32, prev.shape, 0)
    for s in range(1, taps):
        xs = pltpu.roll(x, s, 0)
        head = jnp.where(r8 < s, pltpu.roll(prev, s, 0), xs[:SUBLANES])
        xs = jnp.concatenate([head, xs[SUBLANES:]], axis=0)
        acc = acc + xs * cw[taps - 1 - s:taps - s, :]
    qk = acc * jax.nn.sigmoid(acc)
    q = qk[:, :B_WIDTH]
    k = qk[:, B_WIDTH:] * (B_HEAD_DIM ** -0.5)
    v = pb[:, 2 * B_WIDTH:3 * B_WIDTH]
    o = pb[:, 3 * B_WIDTH:]

    pif = pif_ref[...] + bifr_ref[...]
    lane = lax.broadcasted_iota(jnp.int32, pif.shape, 1)
    logf_c = jnp.where(lane >= B_HEADS, _log_sigmoid(pif), 0.0)
    f_c = _mm32(ltri_ref[...], logf_c)
    g = gt_ref[0] + bifc_ref[...]
    row = lax.broadcasted_iota(jnp.int32, g.shape, 0)
    logf_r = jnp.where(row >= B_HEADS, _log_sigmoid(g), 0.0)
    f_r = _mm32(logf_r, utri_ref[...])

    ti = lax.broadcasted_iota(jnp.int32, (L, L), 0)
    si = lax.broadcasted_iota(jnp.int32, (L, L), 1)
    causal = si <= ti

    W = B_WIDTH
    lane_head = lax.broadcasted_iota(jnp.int32, (L, W), 1) // B_HEAD_DIM
    lane_head_row = lax.broadcasted_iota(jnp.int32, (1, W), 1) // B_HEAD_DIM
    row_head_col = lax.broadcasted_iota(jnp.int32, (W, 1), 0) // B_HEAD_DIM
    same_head = (lax.broadcasted_iota(jnp.int32, (W, W), 0) // B_HEAD_DIM
                 == lax.broadcasted_iota(jnp.int32, (W, W), 1) // B_HEAD_DIM)
    c0 = c_ref[...]
    n0 = n_ref[...]
    q_c = _mm_nt(q, c0)
    q_n = q * n0
    inv_d = 1.0 / B_HEAD_DIM

    hh = jnp.zeros((L, W), F32)
    w_all = jnp.zeros((L, W), F32)
    cd_lane = jnp.zeros((1, W), F32)
    cd_rows = jnp.zeros((W, 1), F32)
    for h in range(B_HEADS):
        hm = lane_head == h
        fc = f_c[:, B_HEADS + h:B_HEADS + h + 1]
        igc = pif[:, h:h + 1]
        fr = f_r[B_HEADS + h:B_HEADS + h + 1, :]
        igr = g[h:h + 1, :]
        f_last = fr[:, L - 1:L]
        m0 = m_ref[h]

        d = jnp.where(causal, fc - fr + igr, -jnp.inf)
        g_inter = fc + m0
        m_t = jnp.maximum(g_inter, jnp.max(d, axis=-1, keepdims=True))
        sw = _mm_nt(jnp.where(hm, q, 0.0), k) * jnp.exp(d - m_t)
        inter = jnp.exp(g_inter - m_t)
        den = (jnp.sum(sw, axis=-1, keepdims=True)
               + inter * jnp.sum(jnp.where(hm, q_n, 0.0), axis=-1, keepdims=True))
        inv = 1.0 / jnp.maximum(jnp.abs(den), jnp.exp(-m_t))
        hh = jnp.where(hm, (_mm(sw, v) + inter * q_c) * inv, hh)

        g_end_r = f_last - fr + igr
        m_new = jnp.maximum(f_last + m0, jnp.max(g_end_r, axis=-1, keepdims=True))
        carry_decay = jnp.exp(f_last + m0 - m_new)
        w_all = jnp.where(hm, jnp.exp(f_last - fc + igc - m_new), w_all)
        cd_lane = jnp.where(lane_head_row == h, carry_decay, cd_lane)
        cd_rows = jnp.where(row_head_col == h, carry_decay, cd_rows)
        m_ref[h] = m_new

    sq = hh * hh
    ms = jnp.zeros((L, W), F32)
    for h in range(B_HEADS):
        hm = lane_head == h
        ms = jnp.where(hm, jnp.sum(jnp.where(hm, sq, 0.0), axis=-1, keepdims=True) * inv_d, ms)
    o_ref[...] = hh * lax.rsqrt(ms + EPS) * hng_ref[...] * jax.nn.sigmoid(o)

    c_ref[...] = cd_rows * c0 + jnp.where(same_head, _mm_tn(v * w_all, k), 0.0)
    n_ref[...] = cd_lane * n0 + jnp.sum(k * w_all, axis=0, keepdims=True)


def _mlstm(p_b, p_if, g_t, conv_w, conv_b, bif_row, bif_col, hn_g, ltri, utri, seq):
    nt = p_b.shape[0]
    L = B_CHUNK
    nc = seq // L
    return pl.pallas_call(
        functools.partial(_mlstm_kernel, chunks_per_seq=nc),
        grid=(nt // L,),
        in_specs=[pl.BlockSpec((L, 4 * B_WIDTH), lambda i: (i, 0)),
                  pl.BlockSpec((SUBLANES, 4 * B_WIDTH), lambda i: (jnp.maximum(i * (L // SUBLANES) - 1, 0), 0)),
                  pl.BlockSpec((L, LANES), lambda i: (i, 0)),
                  pl.BlockSpec((1, SUBLANES, L), lambda i: (i // nc, 0, i % nc)),
                  _full(conv_w.shape), _full(conv_b.shape), _full(bif_row.shape), _full(bif_col.shape),
                  _full(hn_g.shape), _full(ltri.shape), _full(utri.shape)],
        out_specs=pl.BlockSpec((L, B_WIDTH), lambda i: (i, 0)),
        out_shape=jax.ShapeDtypeStruct((nt, B_WIDTH), F32),
        scratch_shapes=[pltpu.VMEM((B_WIDTH, B_WIDTH), F32),
                        pltpu.VMEM((1, B_WIDTH), F32),
                        pltpu.VMEM((B_HEADS, 1, 1), F32)],
        compiler_params=_params(),
        name="mlstm",
    )(p_b, p_b, p_if, g_t, conv_w, conv_b, bif_row, bif_col, hn_g, ltri, utri)


def _swa_kernel(q_ref, kvc_ref, kvp_ref, qn_ref, kn_ref, sink_ref, o_ref, *, blocks_per_seq):
    i = pl.program_id(0)
    first = (i % blocks_per_seq) == 0
    bk = C_BLOCK
    hp = LANES
    group = C_Q_HEADS // C_KV_HEADS
    rows = group * bk
    kv = jnp.concatenate([kvp_ref[...], kvc_ref[...]], axis=0)
    qrow = lax.broadcasted_iota(jnp.int32, (rows, 2 * bk), 0)
    kj = lax.broadcasted_iota(jnp.int32, (rows, 2 * bk), 1)
    dist = qrow % bk + bk - kj
    valid = (dist >= 0) & (dist < WINDOW) & (kj >= jnp.where(first, bk, 0))
    distf = dist.astype(F32)
    rgroup = lax.broadcasted_iota(jnp.int32, (rows, 1), 0) // bk
    inv_d = 1.0 / C_HEAD_DIM
    qn = qn_ref[...]
    kn = kn_ref[...]

    def per_row_group(vals):
        out = vals[-1]
        for j in range(group - 2, -1, -1):
            out = jnp.where(rgroup == j, vals[j], out)
        return out

    for kvh in range(C_KV_HEADS):
        kh = kv[:, kvh * hp:(kvh + 1) * hp]
        kh = kh * lax.rsqrt(jnp.sum(kh * kh, axis=-1, keepdims=True) * inv_d + EPS) * kn
        vh = kv[:, (C_KV_HEADS + kvh) * hp:(C_KV_HEADS + kvh + 1) * hp]
        heads = [kvh * group + j for j in range(group)]
        q4 = jnp.concatenate([q_ref[:, h * hp:(h + 1) * hp] for h in heads], axis=0)
        q4 = q4 * lax.rsqrt(jnp.sum(q4 * q4, axis=-1, keepdims=True) * inv_d + EPS) * qn * (C_HEAD_DIM ** -0.5)
        slope = per_row_group([float(2.0 ** (-8.0 * (h + 1) / C_Q_HEADS)) for h in heads])
        sink = per_row_group([sink_ref[:, h:h + 1] for h in heads])
        s = jnp.where(valid, _mm_nt(q4, kh) - slope * distf, -jnp.inf)
        m = jnp.maximum(jnp.max(s, axis=-1, keepdims=True), sink)
        p = jnp.exp(s - m)
        den = jnp.sum(p, axis=-1, keepdims=True) + jnp.exp(sink - m)
        out = _mm(p, vh) * (1.0 / den)
        for pair in range(group // 2):
            even = out[(2 * pair) * bk:(2 * pair + 1) * bk]
            odd = out[(2 * pair + 1) * bk:(2 * pair + 2) * bk]
            t = kvh * (group // 2) + pair
            o_ref[:, t * hp:(t + 1) * hp] = even + pltpu.roll(odd, C_HEAD_DIM, 1)


def _swa(p_cq, p_ckv, qn_g, kn_g, sink_row, seq):
    nt = p_cq.shape[0]
    bk = C_BLOCK
    nb = seq // bk
    return pl.pallas_call(
        functools.partial(_swa_kernel, blocks_per_seq=nb),
        grid=(nt // bk,),
        in_specs=[pl.BlockSpec((bk, p_cq.shape[1]), lambda i: (i, 0)),
                  pl.BlockSpec((bk, p_ckv.shape[1]), lambda i: (i, 0)),
                  pl.BlockSpec((bk, p_ckv.shape[1]), lambda i: (jnp.maximum(i - 1, 0), 0)),
                  _full(qn_g.shape), _full(kn_g.shape), _full(sink_row.shape)],
        out_specs=pl.BlockSpec((bk, C_WIDTH), lambda i: (i, 0)),
        out_shape=jax.ShapeDtypeStruct((nt, C_WIDTH), F32),
        compiler_params=_params(),
        name="swa",
    )(p_cq, p_ckv, p_ckv, qn_g, kn_g, sink_row)


def _merge_kernel(x_ref, gt_ref, pg_ref, ya_ref, r_ref, k_ref, v_ref, g_ref, yb_ref, yc_ref,
                  hsum_ref, lng_ref, lnb_ref, rk_ref, wa_ref, wb_ref, wc_ref, wo_ref, o_ref):
    hsum = hsum_ref[...]
    inv_n = 1.0 / A_HEAD_DIM
    y = ya_ref[...]
    mu = _mm32(y, hsum) * inv_n
    dlt = y - mu
    var = _mm32(dlt * dlt, hsum) * inv_n
    yn = dlt * lax.rsqrt(var + A_LN_EPS) * lng_ref[...] + lnb_ref[...]
    bonus = _mm32(r_ref[...] * k_ref[...] * rk_ref[...], hsum) * v_ref[...]
    y_a = (yn + bonus) * g_ref[...]
    d = x_ref.shape[1]
    pg = pg_ref[...]
    merged = (jax.nn.sigmoid(pg[:, :d]) * _mm(y_a, wa_ref[...])
              + jax.nn.sigmoid(pg[:, d:2 * d]) * _mm(yb_ref[...], wb_ref[...])
              + jax.nn.sigmoid(pg[:, 2 * d:]) * _mm(yc_ref[...], wc_ref[...]))
    o_ref[...] = x_ref[...] + gt_ref[0] * _mm(merged, wo_ref[...])


def _merge(x2, gt, p_g, y_scan, r, k, v, g, y_b, y_c, hsum, lng, lnb, rk, wa, wb, wc, wo, seq):
    nt, d = x2.shape
    tm = ROW_TILE
    tps = seq // tm
    row = lambda w: pl.BlockSpec((tm, w), lambda i: (i, 0))
    consts = (hsum, lng, lnb, rk, wa, wb, wc, wo)
    return pl.pallas_call(
        _merge_kernel,
        grid=(nt // tm,),
        in_specs=[row(d), pl.BlockSpec((1, 1, d), lambda i: (i // tps, 0, 0)), row(3 * d)]
                 + [row(A_WIDTH)] * 5 + [row(B_WIDTH), row(C_WIDTH)] + [_full(c.shape) for c in consts],
        out_specs=row(d),
        out_shape=jax.ShapeDtypeStruct((nt, d), F32),
        compiler_params=_params(),
        name="merge",
    )(x2, gt, p_g, y_scan, r, k, v, g, y_b, y_c, *consts)


def _route_kernel(x_ref, g_ref, sc_ref, sh_ref, wr_ref, br_ref, lstrict_ref, h_ref, idx_ref, wgt_ref, cnt_ref):
    x = x_ref[...]
    y = x * lax.rsqrt(jnp.mean(x * x, axis=-1, keepdims=True) + EPS) * g_ref[...]
    h = y * (1.0 + sc_ref[0]) + sh_ref[0]
    h_ref[...] = h
    lg = _mm32(h, wr_ref[...]) + br_ref[...]
    lane = lax.broadcasted_iota(jnp.int32, lg.shape, 1)
    lanef = lane.astype(F32)
    ninf = -jnp.inf
    big = float(LANES)

    def first_argmax(vals, vmax):
        return jnp.min(jnp.where(vals == vmax, lanef, big), axis=-1, keepdims=True)

    gl = jnp.where(lane < N_GROUPS, lg, ninf)
    gmax = jnp.max(gl, axis=-1, keepdims=True)
    g_top = 1.0 / jnp.sum(jnp.exp(gl - gmax), axis=-1, keepdims=True)
    g_idx = first_argmax(gl, gmax)
    lo = N_GROUPS + EXPERTS_PER_GROUP * g_idx
    el = jnp.where((lanef >= lo) & (lanef < lo + EXPERTS_PER_GROUP), lg, ninf)
    e1 = jnp.max(el, axis=-1, keepdims=True)
    z = jnp.sum(jnp.exp(el - e1), axis=-1, keepdims=True)
    i1 = first_argmax(el, e1)
    el2 = jnp.where(lanef == i1, ninf, el)
    e2 = jnp.max(el2, axis=-1, keepdims=True)
    i2 = first_argmax(el2, e2)
    p1 = 1.0 / z
    p2 = jnp.exp(e2 - e1) / z
    w1 = g_top * p1 / (p1 + p2)
    w2 = g_top * p2 / (p1 + p2)
    oh1 = jnp.where(lanef == i1, 1.0, 0.0)
    oh2 = jnp.where(lanef == i2, 1.0, 0.0)
    both = oh1 + oh2
    before = _mm32(lstrict_ref[...], both)
    r1 = jnp.sum(before * oh1, axis=-1, keepdims=True)
    r2 = jnp.sum(before * oh2, axis=-1, keepdims=True)
    idx = jnp.where(lane == 0, i1 - N_GROUPS, jnp.where(lane == 1, i2 - N_GROUPS,
                    jnp.where(lane == 2, r1, jnp.where(lane == 3, r2, 0.0))))
    idx_ref[...] = idx.astype(jnp.int32)
    wgt_ref[...] = jnp.where(lane == 0, w1, jnp.where(lane == 1, w2, 0.0))
    counts = jnp.sum(both, axis=0, keepdims=True)
    cnt_ref[0] = jnp.broadcast_to(counts, cnt_ref.shape[1:]).astype(jnp.int32)


def _route(x2, g, sc, sh, w_r, b_r, seq):
    nt, d = x2.shape
    tm = ROW_TILE
    tps = seq // tm
    n_tiles = nt // tm
    row = lambda w: pl.BlockSpec((tm, w), lambda i: (i, 0))
    mod = pl.BlockSpec((1, 1, d), lambda i: (i // tps, 0, 0))
    lstrict = jnp.tril(jnp.ones((tm, tm), F32), k=-1)
    return pl.pallas_call(
        _route_kernel,
        grid=(n_tiles,),
        in_specs=[row(d), _full((1, d)), mod, mod, _full(w_r.shape), _full(b_r.shape), _full(lstrict.shape)],
        out_specs=[row(d), row(LANES), row(LANES), pl.BlockSpec((1, SUBLANES, LANES), lambda i: (i, 0, 0))],
        out_shape=[jax.ShapeDtypeStruct((nt, d), F32), jax.ShapeDtypeStruct((nt, LANES), jnp.int32),
                   jax.ShapeDtypeStruct((nt, LANES), F32),
                   jax.ShapeDtypeStruct((n_tiles, SUBLANES, LANES), jnp.int32)],
        compiler_params=_params(),
        name="route",
    )(x2, g, sc, sh, w_r, b_r, lstrict)


def _expert_kernel(bexp_ref, tok_ref, tok_next_ref, h_hbm, win_ref, wout_ref, o_ref, buf, sem, win_bf, wout_bf):
    i = pl.program_id(0)
    slot = _pipelined_row_gather(tok_ref, tok_next_ref, MOE_BLOCK, h_hbm, buf, sem)

    @pl.when(jnp.logical_or(i == 0, bexp_ref[i] != bexp_ref[jnp.maximum(i - 1, 0)]))
    def _():
        win_bf[...] = win_ref[0, 0].astype(BF16)
        wout_bf[...] = wout_ref[0, 0].astype(BF16)

    gu = _mm(buf[slot], win_bf[...])
    gate = gu[:, :D_EXPERT]
    up = gu[:, D_EXPERT:]
    o_ref[...] = _mm(gate * jax.nn.sigmoid(gate) * up, wout_bf[...])


def _experts(block_exp, slot_tok, h2, w_in_e, w_out_e, layer):
    n_blocks = block_exp.shape[0]
    rows = MOE_BLOCK
    d = h2.shape[1]
    tok3 = slot_tok.reshape(n_blocks, 1, rows)
    smem_blk = lambda f: pl.BlockSpec((1, 1, rows), f, memory_space=pltpu.SMEM)
    grid_spec = pltpu.PrefetchScalarGridSpec(
        num_scalar_prefetch=1,
        grid=(n_blocks,),
        in_specs=[smem_blk(lambda i, be: (i, 0, 0)),
                  smem_blk(lambda i, be: (jnp.minimum(i + 1, n_blocks - 1), 0, 0)),
                  pl.BlockSpec(memory_space=pl.ANY),
                  pl.BlockSpec((1, 1, d, 2 * D_EXPERT), lambda i, be: (layer, be[i], 0, 0)),
                  pl.BlockSpec((1, 1, D_EXPERT, d), lambda i, be: (layer, be[i], 0, 0))],
        out_specs=pl.BlockSpec((rows, d), lambda i, be: (i, 0)),
        scratch_shapes=[pltpu.VMEM((2, rows, d), F32), pltpu.SemaphoreType.DMA((2,)),
                        pltpu.VMEM((d, 2 * D_EXPERT), BF16), pltpu.VMEM((D_EXPERT, d), BF16)],
    )
    return pl.pallas_call(
        _expert_kernel,
        grid_spec=grid_spec,
        out_shape=jax.ShapeDtypeStruct((n_blocks * rows, d), F32),
        compiler_params=_params(),
        name="experts",
    )(block_exp, tok3, tok3, h2, w_in_e, w_out_e)


def _combine_kernel(pos_ref, pos_next_ref, x_ref, gt_ref, wgt_ref, yb_hbm, o_ref, buf, sem):
    o_ref[...] = _moe_residual(pos_ref, pos_next_ref, x_ref, gt_ref, wgt_ref, yb_hbm, buf, sem)


def _combine(pos, x2, gt, wgt, yb, seq):
    nt, d = x2.shape
    tm = ROW_TILE
    n_tiles = nt // tm
    pos3, specs, scratch = _moe_residual_specs(pos, tm, n_tiles, d, seq // tm)
    return pl.pallas_call(
        _combine_kernel,
        grid=(n_tiles,),
        in_specs=specs,
        out_specs=pl.BlockSpec((tm, d), lambda i: (i, 0)),
        out_shape=jax.ShapeDtypeStruct((nt, d), F32),
        scratch_shapes=scratch,
        compiler_params=_params(),
        name="combine",
    )(pos3, pos3, x2, gt, wgt, yb)


def _dispatch_plan(ridx, tile_counts, n_tokens):
    n_tiles = tile_counts.shape[0]
    tm = n_tokens // n_tiles
    na = n_tokens * TOP_K
    counts = tile_counts[:, 0, N_GROUPS:N_GROUPS + N_EXPERTS]
    tile_off = jnp.cumsum(counts, axis=0) - counts
    total = jnp.sum(counts, axis=0)
    padded = (total + MOE_BLOCK - 1) // MOE_BLOCK * MOE_BLOCK
    pad_ends = jnp.cumsum(padded)
    base = (pad_ends - padded)[None, :] + tile_off
    expert = ridx[:, :TOP_K].reshape(n_tiles, tm, TOP_K)
    rank = ridx[:, TOP_K:2 * TOP_K].reshape(n_tiles, tm, TOP_K)
    onehot = expert[..., None] == jnp.arange(N_EXPERTS, dtype=jnp.int32)
    dest = jnp.sum(jnp.where(onehot, base[:, None, None, :], 0), axis=-1) + rank
    dest = dest.reshape(n_tokens, TOP_K)
    n_blocks = -(-na // MOE_BLOCK) + N_EXPERTS
    slot_tok = jnp.zeros((n_blocks * MOE_BLOCK,), jnp.int32).at[dest.reshape(-1)].set(
        jnp.arange(na, dtype=jnp.int32) // TOP_K)
    block_start = jnp.arange(n_blocks, dtype=jnp.int32) * MOE_BLOCK
    block_exp = jnp.minimum(jnp.sum((pad_ends[None, :] <= block_start[:, None]).astype(jnp.int32), axis=1),
                            N_EXPERTS - 1)
    return block_exp, slot_tok, dest


def _pad_cols(w, width):
    return jnp.pad(w, ((0, 0), (0, width - w.shape[1])))


def _pad_heads(w, head_dim):
    rows = w.shape[0]
    w = w.reshape(rows, -1, head_dim)
    return jnp.pad(w, ((0, 0), (0, 0), (0, LANES - head_dim))).reshape(rows, -1)


def kernel(x, c, w_ada, b_ada, norm1_g, norm2_g, w_in, mu_a, w0_a, w2_a, a0_a, a2_a, g2_a, kk_a, ka_a, rk_a,
           lnx_g_a, lnx_b_a, conv_w_b, conv_b_b, bi_b, bf_b, hn_g_b, qn_g_c, kn_g_c, sink_c, w_br_a, w_br_b,
           w_br_c, w_out, w_rg, b_rg, w_re, b_re, w_e_in, w_e_out):
    bsz, seq, d = x.shape
    nt = bsz * seq
    depth = w_in.shape[0]
    assert bsz * A_HEADS * VAL_LANE_REPS == LANES, "the RWKV recurrence packs (batch, head, 4 value indices) on lanes"
    assert seq % ROW_TILE == 0 and seq % B_CHUNK == 0 and seq % SCAN_STEPS == 0

    mod = _ada_mod(c, w_ada, b_ada)
    hsum = jnp.kron(jnp.eye(A_HEADS, dtype=F32), jnp.ones((A_HEAD_DIM, A_HEAD_DIM), F32))
    tri = jnp.tril(jnp.ones((B_CHUNK, B_CHUNK), F32))

    o_b = 1024
    o_if = o_b + 4 * B_WIDTH
    o_cq = o_if + 2 * B_HEADS
    o_ckv = o_cq + C_WIDTH
    o_g = o_ckv + 2 * C_KV_WIDTH

    x2 = x.reshape(nt, d)
    moe = None
    for l in range(depth):
        m = mod[l].reshape(bsz, 6, 1, d)
        sh1, sc1, gt1, sh2, sc2, gt2 = (m[:, i] for i in range(6))
        wl = w_in[l]
        weights = [wl[:, o_g:].astype(BF16), wl[:, :o_b].astype(BF16), wl[:, o_b:o_if].astype(BF16),
                   _pad_heads(wl[:, o_cq:o_ckv], C_HEAD_DIM).astype(BF16),
                   _pad_heads(wl[:, o_ckv:o_g], C_HEAD_DIM).astype(BF16),
                   _pad_cols(wl[:, o_if:o_cq], LANES).astype(BF16)]
        outs = _norm_proj(x2, norm1_g[l][None], sc1, sh1, weights, seq, moe)
        if moe is not None:
            x2 = outs.pop(0)
        p_g, p_a, p_b, p_cq, p_ckv, p_if = outs

        wwa = (jnp.zeros((A_DECAY_LORA + A_ICL_LORA, 2 * A_WIDTH), F32)
               .at[:A_DECAY_LORA, :A_WIDTH].set(w2_a[l]).at[A_DECAY_LORA:, A_WIDTH:].set(a2_a[l]))
        r, dec, k2, v, kk, bvec, g = _rwkv_prep(p_a, mu_a[l][None], wwa, w0_a[l][None], a0_a[l][None], g2_a[l],
                                                kk_a[l][None], ka_a[l][None], hsum, seq)
        y_scan = _rwkv_scan(*(_key_layout(t, bsz, seq) for t in (dec, kk, bvec, k2, r)),
                            _value_layout(v, bsz, seq))
        y_scan = _value_unlayout(y_scan, bsz, seq)

        g_t = p_if[:, :SUBLANES].reshape(bsz, seq, SUBLANES).transpose(0, 2, 1)
        bif = jnp.concatenate([bi_b[l], bf_b[l]])
        y_b = _mlstm(p_b, p_if, g_t, conv_w_b[l], conv_b_b[l][None], _pad_cols(bif[None], LANES), bif[:, None],
                     hn_g_b[l].reshape(1, B_WIDTH), tri, tri.T, seq)

        y_c = _swa(p_cq, p_ckv, _pad_cols(qn_g_c[l][None], LANES), _pad_cols(kn_g_c[l][None], LANES),
                   _pad_cols(sink_c[l][None], LANES), seq)

        x2 = _merge(x2, gt1, p_g, y_scan, r, k2, v, g, y_b, y_c, hsum, lnx_g_a[l][None], lnx_b_a[l][None],
                    rk_a[l].reshape(1, A_WIDTH), w_br_a[l].astype(BF16), w_br_b[l].astype(BF16),
                    w_br_c[l].astype(BF16), w_out[l].astype(BF16), seq)

        w_r = _pad_cols(jnp.concatenate([w_rg[l], w_re[l]], axis=1), LANES)
        b_r = _pad_cols(jnp.concatenate([b_rg[l], b_re[l]])[None], LANES)
        h2, ridx, rwgt, tile_counts = _route(x2, norm2_g[l][None], sc2, sh2, w_r, b_r, seq)
        block_exp, slot_tok, pos = _dispatch_plan(ridx, tile_counts, nt)
        yb = _experts(block_exp, slot_tok, h2, w_e_in, w_e_out, l)
        moe = (pos, gt2, rwgt, yb)
    pos, gt2, rwgt, yb = moe
    x2 = _combine(pos, x2, gt2, rwgt, yb, seq)
    return x2.reshape(bsz, seq, d)
```

```python
import functools

import numpy as np
import jax
import jax.numpy as jnp
from jax import lax
from jax.experimental import pallas as pl
from jax.experimental.pallas import tpu as pltpu

F32 = jnp.float32
BF16 = jnp.bfloat16

A_HEADS = 4
A_HEAD_DIM = 64
A_WIDTH = 256
A_DECAY_LORA = 64
A_ICL_LORA = 64
A_LN_EPS = 64e-5
B_HEADS = 4
B_HEAD_DIM = 64
B_WIDTH = 256
B_CHUNK = 128
C_Q_HEADS = 8
C_KV_HEADS = 2
C_HEAD_DIM = 64
C_WIDTH = 512
C_KV_WIDTH = 128
WINDOW = 128
C_BLOCK = 128
N_GROUPS = 4
EXPERTS_PER_GROUP = 8
N_EXPERTS = 32
TOP_K = 2
D_EXPERT = 512
MOE_BLOCK = 128
EPS = 1e-6

LANES = 128
SUBLANES = 8
VMEM_LIMIT = 56 * 1024 * 1024

ROW_TILE = 256
SCAN_STEPS = 32
ADA_COL_TILE = 1536


def _mm(a, b):
    return jnp.dot(a.astype(BF16), b.astype(BF16), preferred_element_type=F32)


def _bf16_terms(x, n):
    terms = []
    for t in range(n):
        p = x.astype(BF16)
        terms.append(p)
        if t + 1 < n:
            x = x - p.astype(F32)
    return terms


def _sum_dots(pairs):
    out = None
    for a, b in reversed(pairs):
        t = jnp.dot(a, b, preferred_element_type=F32)
        out = t if out is None else out + t
    return out


def _mm_f32_lhs(a, b01):
    b = b01.astype(BF16)
    return _sum_dots([(p, b) for p in _bf16_terms(a, 3)])


def _mm_f32_rhs(a01, b):
    a = a01.astype(BF16)
    return _sum_dots([(a, p) for p in _bf16_terms(b, 3)])


def _mm3(a, b):
    a_hi, a_lo = _bf16_terms(a, 2)
    b_hi, b_lo = _bf16_terms(b, 2)
    return _sum_dots([(a_hi, b_hi), (a_hi, b_lo), (a_lo, b_hi)])


def _mm_nt(a, b):
    return lax.dot_general(a.astype(BF16), b.astype(BF16), (((1,), (1,)), ((), ())),
                           preferred_element_type=F32)


def _mm_tn(a, b):
    return lax.dot_general(a.astype(BF16), b.astype(BF16), (((0,), (0,)), ((), ())),
                           preferred_element_type=F32)


def _log_sigmoid(x):
    return jnp.minimum(x, 0.0) - jnp.log(1.0 + jnp.exp(-jnp.abs(x)))


def _params(n_axes=1):
    return pltpu.CompilerParams(dimension_semantics=("arbitrary",) * n_axes,
                                vmem_limit_bytes=VMEM_LIMIT)


def _full(shape):
    nd = len(shape)
    return pl.BlockSpec(shape, lambda *_: (0,) * nd)


def _ada_kernel(c_ref, w_ref, b_ref, o_ref):
    c = c_ref[...]
    cond = c * jax.nn.sigmoid(c)
    o_ref[0] = _mm(cond, w_ref[0]) + b_ref[0]


def _ada_mod(c, w_ada, b_ada):
    depth, d, n = w_ada.shape
    bsz = c.shape[0]
    tn = ADA_COL_TILE
    return pl.pallas_call(
        _ada_kernel,
        grid=(depth, n // tn),
        in_specs=[pl.BlockSpec((bsz, d), lambda l, j: (0, 0)),
                  pl.BlockSpec((1, d, tn), lambda l, j: (l, 0, j)),
                  pl.BlockSpec((1, 1, tn), lambda l, j: (l, 0, j))],
        out_specs=pl.BlockSpec((1, bsz, tn), lambda l, j: (l, 0, j)),
        out_shape=jax.ShapeDtypeStruct((depth, bsz, n), F32),
        compiler_params=_params(2),
        name="ada_mod",
    )(c, w_ada, b_ada.reshape(depth, 1, n))


TILE_SLABS = SUBLANES


def _slab(ref, first_token, n_tokens, s):
    return ref[pl.ds(first_token * TILE_SLABS + s, n_tokens, stride=TILE_SLABS), :]


def _store_token_tiles(ref, x):
    n = x.shape[0]
    for s in range(TILE_SLABS):
        ref[pl.ds(s, n, stride=TILE_SLABS), :] = x[:, s * LANES:(s + 1) * LANES]


def _start_row_gather(idx_ref, n_rows, src_hbm, dst, sem):
    for r in range(n_rows):
        first = pl.multiple_of(idx_ref[0, 0, r], TILE_SLABS)
        pltpu.make_async_copy(src_hbm.at[pl.ds(first, TILE_SLABS), :], dst.at[pl.ds(r * TILE_SLABS, TILE_SLABS), :],
                              sem).start(priority=r % 2)


def _wait_row_gather(n_rows, src_hbm, dst, sem):
    pltpu.make_async_copy(src_hbm.at[pl.ds(0, n_rows * TILE_SLABS), :], dst, sem).wait()


def _pipelined_row_gather(idx_ref, idx_next_ref, n_rows, src_hbm, buf, sem):
    i = pl.program_id(0)
    slot = i % 2

    @pl.when(i == 0)
    def _():
        _start_row_gather(idx_ref, n_rows, src_hbm, buf.at[0], sem.at[0])

    _start_row_gather(idx_next_ref, n_rows, src_hbm, buf.at[1 - slot], sem.at[1 - slot])
    _wait_row_gather(n_rows, src_hbm, buf.at[slot], sem.at[slot])

    @pl.when(i == pl.num_programs(0) - 1)
    def _():
        _wait_row_gather(n_rows, src_hbm, buf.at[1 - slot], sem.at[1 - slot])

    return slot


def _moe_residual(pos_ref, pos_next_ref, x_ref, gt_ref, wgt_ref, yb_hbm, buf, sem):
    tm = x_ref.shape[0]
    slot = _pipelined_row_gather(pos_ref, pos_next_ref, TOP_K * tm, yb_hbm, buf, sem)
    wgt = wgt_ref[...]
    w1, w2 = wgt[:, 0:1], wgt[:, 1:2]
    rows = buf.at[slot]
    gt = gt_ref[0]
    slabs = []
    for s in range(TILE_SLABS):
        lanes = slice(s * LANES, (s + 1) * LANES)
        y = _slab(rows, 0, tm, s) * w1 + _slab(rows, tm, tm, s) * w2
        slabs.append(x_ref[:, lanes] + gt[:, lanes] * y)
    return jnp.concatenate(slabs, axis=1)


def _moe_residual_specs(pos, tm, n_tiles, d, tps):
    pos3 = pos.reshape(n_tiles, tm, TOP_K).transpose(0, 2, 1).reshape(n_tiles, 1, TOP_K * tm)
    smem_blk = lambda f: pl.BlockSpec((1, 1, TOP_K * tm), f, memory_space=pltpu.SMEM)
    row = lambda w: pl.BlockSpec((tm, w), lambda i: (i, 0))
    specs = [smem_blk(lambda i: (i, 0, 0)),
             smem_blk(lambda i: (jnp.minimum(i + 1, n_tiles - 1), 0, 0)),
             row(d), pl.BlockSpec((1, 1, d), lambda i: (i // tps, 0, 0)), row(LANES),
             pl.BlockSpec(memory_space=pl.ANY)]
    assert d == TILE_SLABS * LANES
    scratch = [pltpu.VMEM((2, TOP_K * tm * TILE_SLABS, LANES), F32), pltpu.SemaphoreType.DMA((2,))]
    return pos3, specs, scratch


def _norm_proj_kernel(*refs, n_proj, moe):
    if moe:
        x = _moe_residual(*refs[:6], *refs[-2:])
        refs = refs[6:-2]
        refs[-1][...] = x
        refs = refs[:-1]
    else:
        x = refs[0][...]
        refs = refs[1:]
    g_ref, sc_ref, sh_ref = refs[:3]
    y = x * lax.rsqrt(jnp.mean(x * x, axis=-1, keepdims=True) + EPS) * g_ref[...]
    h = (y * (1.0 + sc_ref[0]) + sh_ref[0]).astype(BF16)
    for w_ref, o_ref in zip(refs[3:3 + n_proj], refs[3 + n_proj:]):
        o_ref[...] = jnp.dot(h, w_ref[...], preferred_element_type=F32)


def _norm_proj(x2, g, sc, sh, weights, seq, moe=None):
    nt, d = x2.shape
    tm = ROW_TILE
    tps = seq // tm
    n_tiles = nt // tm
    row = lambda w: pl.BlockSpec((tm, w), lambda i: (i, 0))
    mod = pl.BlockSpec((1, 1, d), lambda i: (i // tps, 0, 0))
    in_specs = [_full((1, d)), mod, mod] + [_full(w.shape) for w in weights]
    out_specs = [row(w.shape[1]) for w in weights]
    out_shape = [jax.ShapeDtypeStruct((nt, w.shape[1]), F32) for w in weights]
    if moe is None:
        args, head_specs, scratch = [x2], [row(d)], []
    else:
        pos, gt, wgt, yb = moe
        pos3, head_specs, scratch = _moe_residual_specs(pos, tm, n_tiles, d, tps)
        args = [pos3, pos3, x2, gt, wgt, yb]
        out_specs.append(row(d))
        out_shape.append(jax.ShapeDtypeStruct((nt, d), F32))
    outs = pl.pallas_call(
        functools.partial(_norm_proj_kernel, n_proj=len(weights), moe=moe is not None),
        grid=(n_tiles,),
        in_specs=head_specs + in_specs,
        out_specs=out_specs,
        out_shape=out_shape,
        scratch_shapes=scratch,
        compiler_params=_params(),
        name="norm_proj_moe" if moe is not None else "norm_proj",
    )(*args, g, sc, sh, *weights)
    outs = list(outs)
    return outs if moe is None else [outs[-1]] + outs[:-1]


def _rwkv_prep_kernel(p_ref, prev_ref, mu_ref, wwa_ref, w0_ref, a0_ref, g2_ref, kkw_ref, kaw_ref, hsum_ref,
                      r_out, w_out, k_out, v_out, kk_out, b_out, g_out, *, tiles_per_seq):
    i = pl.program_id(0)
    p = p_ref[...]
    first = (i % tiles_per_seq) == 0
    prev_row = jnp.where(first, 0.0, prev_ref[SUBLANES - 1:SUBLANES, :])
    row = lax.broadcasted_iota(jnp.int32, p.shape, 0)
    shifted = jnp.where(row == 0, prev_row, pltpu.roll(p, 1, 0))
    pa = p + mu_ref[...] * (shifted - p)
    aw = A_WIDTH
    o_lora = 3 * aw
    o_glo = o_lora + A_DECAY_LORA + A_ICL_LORA
    r = pa[:, 0:aw]
    k = pa[:, aw:2 * aw]
    v = pa[:, 2 * aw:o_lora]
    slab = pa[:, o_lora:o_glo]
    glo = pa[:, o_glo:]
    lane = lax.broadcasted_iota(jnp.int32, slab.shape, 1)
    z = jnp.where(lane < A_DECAY_LORA, jnp.tanh(slab), slab)
    wa = _mm3(z, wwa_ref[...])
    w_log = _log_sigmoid(w0_ref[...] + wa[:, :aw]) - 0.5
    decay = jnp.exp(-jnp.exp(w_log))
    a = jax.nn.sigmoid(a0_ref[...] + wa[:, aw:])
    g = _mm3(jax.nn.sigmoid(glo), g2_ref[...])
    kk = k * kkw_ref[...]
    ss = _mm_f32_lhs(kk * kk, hsum_ref[...])
    kk = kk / jnp.maximum(jnp.sqrt(ss), 1e-12)
    r_out[...] = r
    w_out[...] = decay
    k_out[...] = k * (1.0 + (a - 1.0) * kaw_ref[...])
    v_out[...] = v
    kk_out[...] = kk
    b_out[...] = kk * a
    g_out[...] = g


def _rwkv_prep(p_a, mu, wwa, w0, a0, g2, kkw, kaw, hsum, seq):
    nt, wd = p_a.shape
    tm = min(ROW_TILE, seq)
    tps = seq // tm
    row = lambda w: pl.BlockSpec((tm, w), lambda i: (i, 0))
    prev = pl.BlockSpec((SUBLANES, wd), lambda i: (jnp.maximum(i * (tm // SUBLANES) - 1, 0), 0))
    outs = [jax.ShapeDtypeStruct((nt, A_WIDTH), F32)] * 7
    return pl.pallas_call(
        functools.partial(_rwkv_prep_kernel, tiles_per_seq=tps),
        grid=(nt // tm,),
        in_specs=[row(wd), prev, _full(mu.shape), _full(wwa.shape), _full(w0.shape), _full(a0.shape),
                  _full(g2.shape), _full(kkw.shape), _full(kaw.shape), _full(hsum.shape)],
        out_specs=[row(A_WIDTH)] * 7,
        out_shape=outs,
        compiler_params=_params(),
        name="rwkv_prep",
    )(p_a, p_a, mu, wwa, w0, a0, g2, kkw, kaw, hsum)


VAL_LANE_REPS = 4
VAL_TILES = A_HEAD_DIM // (SUBLANES * VAL_LANE_REPS)
SUM_CHAINS = 4


def _rwkv_scan_kernel(w_ref, kk_ref, b_ref, k_ref, r_ref, v_ref, y_ref, s_ref, *, steps):
    @pl.when(pl.program_id(0) == 0)
    def _():
        s_ref[...] = jnp.zeros_like(s_ref)

    def tree(terms):
        while len(terms) > 1:
            terms = [terms[i] + terms[i + 1] for i in range(0, len(terms), 2)]
        return terms[0]

    def step(t, carry):
        def row(ref, j):
            return jnp.broadcast_to(ref[t, j:j + 1, :], (SUBLANES, LANES))

        tiles = range(VAL_TILES)
        v_t = [v_ref[t, ih * SUBLANES:(ih + 1) * SUBLANES, :] for ih in tiles]
        acc = [[None] * SUM_CHAINS for _ in tiles]
        for j in range(A_HEAD_DIM):
            kk = row(kk_ref, j)
            for ih in tiles:
                term = s_ref[ih * A_HEAD_DIM + j] * kk
                c = j % SUM_CHAINS
                acc[ih][c] = term if acc[ih][c] is None else acc[ih][c] + term
        sa = [tree(a) for a in acc]
        y = [[None] * SUM_CHAINS for _ in tiles]
        for j in range(A_HEAD_DIM):
            w, b, k, r = (row(ref, j) for ref in (w_ref, b_ref, k_ref, r_ref))
            for ih in tiles:
                idx = ih * A_HEAD_DIM + j
                s = s_ref[idx] * w - sa[ih] * b + v_t[ih] * k
                s_ref[idx] = s
                term = s * r
                c = j % SUM_CHAINS
                y[ih][c] = term if y[ih][c] is None else y[ih][c] + term
        for ih in tiles:
            y_ref[t, ih * SUBLANES:(ih + 1) * SUBLANES, :] = tree(y[ih])
        return carry

    lax.fori_loop(0, steps, step, 0)


def _rwkv_scan(w_e, kk_e, b_e, k_e, r_e, v_p):
    seq = w_e.shape[0]
    tc = SCAN_STEPS
    key = pl.BlockSpec((tc, A_HEAD_DIM, LANES), lambda i: (i, 0, 0))
    val = pl.BlockSpec((tc, VAL_TILES * SUBLANES, LANES), lambda i: (i, 0, 0))
    return pl.pallas_call(
        functools.partial(_rwkv_scan_kernel, steps=tc),
        grid=(seq // tc,),
        in_specs=[key] * 5 + [val],
        out_specs=val,
        out_shape=jax.ShapeDtypeStruct((seq, VAL_TILES * SUBLANES, LANES), F32),
        scratch_shapes=[pltpu.VMEM((VAL_TILES * A_HEAD_DIM, SUBLANES, LANES), F32)],
        compiler_params=_params(),
        name="rwkv_scan",
    )(w_e, kk_e, b_e, k_e, r_e, v_p)


def _key_layout(x, bsz, seq):
    x = x.reshape(bsz, seq, A_HEADS, A_HEAD_DIM).transpose(1, 3, 0, 2)
    x = jnp.broadcast_to(x[..., None], (seq, A_HEAD_DIM, bsz, A_HEADS, VAL_LANE_REPS))
    return x.reshape(seq, A_HEAD_DIM, LANES)


def _value_layout(x, bsz, seq):
    x = x.reshape(bsz, seq, A_HEADS, VAL_TILES, SUBLANES, VAL_LANE_REPS).transpose(1, 3, 4, 0, 2, 5)
    return x.reshape(seq, VAL_TILES * SUBLANES, LANES)


def _value_unlayout(y, bsz, seq):
    y = y.reshape(seq, VAL_TILES, SUBLANES, bsz, A_HEADS, VAL_LANE_REPS).transpose(3, 0, 4, 1, 2, 5)
    return y.reshape(bsz * seq, A_WIDTH)


def _mlstm_kernel(pb_ref, prev_ref, pif_ref, gt_ref, cw_ref, cb_ref, bifr_ref, bifc_ref, hng_ref,
                  ltri_ref, utri_ref, o_ref, c_ref, n_ref, m_ref, *, chunks_per_seq):
    i = pl.program_id(0)
    first = (i % chunks_per_seq) == 0

    @pl.when(first)
    def _():
        c_ref[...] = jnp.zeros_like(c_ref)
        n_ref[...] = jnp.zeros_like(n_ref)
        m_ref[...] = jnp.zeros_like(m_ref)

    L = B_CHUNK
    pb = pb_ref[...]
    x = pb[:, :2 * B_WIDTH]
    prev = jnp.where(first, 0.0, prev_ref[:, :2 * B_WIDTH])
    cw = cw_ref[...]
    taps = cw.shape[0]
    acc = x * cw[taps - 1:taps, :] + cb_ref[...]
    r8 = lax.broadcasted_iota(jnp.int32, prev.shape, 0)
    for s in range(1, taps):
        xs = pltpu.roll(x, s, 0)
        head = jnp.where(r8 < s, pltpu.roll(prev, s, 0), xs[:SUBLANES])
        xs = jnp.concatenate([head, xs[SUBLANES:]], axis=0)
        acc = acc + xs * cw[taps - 1 - s:taps - s, :]
    qk = acc * jax.nn.sigmoid(acc)
    q = qk[:, :B_WIDTH]
    k = qk[:, B_WIDTH:] * (B_HEAD_DIM ** -0.5)
    v = pb[:, 2 * B_WIDTH:3 * B_WIDTH]
    o = pb[:, 3 * B_WIDTH:]

    pif = pif_ref[...] + bifr_ref[...]
    lane = lax.broadcasted_iota(jnp.int32, pif.shape, 1)
    logf_c = jnp.where(lane >= B_HEADS, _log_sigmoid(pif), 0.0)
    f_c = _mm_f32_rhs(ltri_ref[...], logf_c)
    g = gt_ref[0] + bifc_ref[...]
    row = lax.broadcasted_iota(jnp.int32, g.shape, 0)
    logf_r = jnp.where(row >= B_HEADS, _log_sigmoid(g), 0.0)
    f_r = _mm_f32_lhs(logf_r, utri_ref[...])

    ti = lax.broadcasted_iota(jnp.int32, (L, L), 0)
    si = lax.broadcasted_iota(jnp.int32, (L, L), 1)
    causal = si <= ti

    W = B_WIDTH
    lane_head = lax.broadcasted_iota(jnp.int32, (L, W), 1) // B_HEAD_DIM
    lane_head_row = lax.broadcasted_iota(jnp.int32, (1, W), 1) // B_HEAD_DIM
    row_head_col = lax.broadcasted_iota(jnp.int32, (W, 1), 0) // B_HEAD_DIM
    same_head = (lax.broadcasted_iota(jnp.int32, (W, W), 0) // B_HEAD_DIM
                 == lax.broadcasted_iota(jnp.int32, (W, W), 1) // B_HEAD_DIM)
    c0 = c_ref[...]
    n0 = n_ref[...]
    q_c = _mm_nt(q, c0)
    q_n = q * n0
    inv_d = 1.0 / B_HEAD_DIM

    hh = jnp.zeros((L, W), F32)
    w_all = jnp.zeros((L, W), F32)
    cd_lane = jnp.zeros((1, W), F32)
    cd_rows = jnp.zeros((W, 1), F32)
    for h in range(B_HEADS):
        hm = lane_head == h
        fc = f_c[:, B_HEADS + h:B_HEADS + h + 1]
        igc = pif[:, h:h + 1]
        fr = f_r[B_HEADS + h:B_HEADS + h + 1, :]
        igr = g[h:h + 1, :]
        f_last = fr[:, L - 1:L]
        m0 = m_ref[h]

        d = jnp.where(causal, fc - fr + igr, -jnp.inf)
        g_inter = fc + m0
        m_t = jnp.maximum(g_inter, jnp.max(d, axis=-1, keepdims=True))
        sw = _mm_nt(jnp.where(hm, q, 0.0), k) * jnp.exp(d - m_t)
        inter = jnp.exp(g_inter - m_t)
        den = (jnp.sum(sw, axis=-1, keepdims=True)
               + inter * jnp.sum(jnp.where(hm, q_n, 0.0), axis=-1, keepdims=True))
        inv = 1.0 / jnp.maximum(jnp.abs(den), jnp.exp(-m_t))
        hh = jnp.where(hm, (_mm(sw, v) + inter * q_c) * inv, hh)

        g_end_r = f_last - fr + igr
        m_new = jnp.maximum(f_last + m0, jnp.max(g_end_r, axis=-1, keepdims=True))
        carry_decay = jnp.exp(f_last + m0 - m_new)
        w_all = jnp.where(hm, jnp.exp(f_last - fc + igc - m_new), w_all)
        cd_lane = jnp.where(lane_head_row == h, carry_decay, cd_lane)
        cd_rows = jnp.where(row_head_col == h, carry_decay, cd_rows)
        m_ref[h] = m_new

    sq = hh * hh
    ms = jnp.zeros((L, W), F32)
    for h in range(B_HEADS):
        hm = lane_head == h
        ms = jnp.where(hm, jnp.sum(jnp.where(hm, sq, 0.0), axis=-1, keepdims=True) * inv_d, ms)
    o_ref[...] = hh * lax.rsqrt(ms + EPS) * hng_ref[...] * jax.nn.sigmoid(o)

    c_ref[...] = cd_rows * c0 + jnp.where(same_head, _mm_tn(v * w_all, k), 0.0)
    n_ref[...] = cd_lane * n0 + jnp.sum(k * w_all, axis=0, keepdims=True)


def _mlstm(p_b, p_if, g_t, conv_w, conv_b, bif_row, bif_col, hn_g, ltri, utri, seq):
    nt = p_b.shape[0]
    L = B_CHUNK
    nc = seq // L
    return pl.pallas_call(
        functools.partial(_mlstm_kernel, chunks_per_seq=nc),
        grid=(nt // L,),
        in_specs=[pl.BlockSpec((L, 4 * B_WIDTH), lambda i: (i, 0)),
                  pl.BlockSpec((SUBLANES, 4 * B_WIDTH), lambda i: (jnp.maximum(i * (L // SUBLANES) - 1, 0), 0)),
                  pl.BlockSpec((L, LANES), lambda i: (i, 0)),
                  pl.BlockSpec((1, SUBLANES, L), lambda i: (i // nc, 0, i % nc)),
                  _full(conv_w.shape), _full(conv_b.shape), _full(bif_row.shape), _full(bif_col.shape),
                  _full(hn_g.shape), _full(ltri.shape), _full(utri.shape)],
        out_specs=pl.BlockSpec((L, B_WIDTH), lambda i: (i, 0)),
        out_shape=jax.ShapeDtypeStruct((nt, B_WIDTH), F32),
        scratch_shapes=[pltpu.VMEM((B_WIDTH, B_WIDTH), F32),
                        pltpu.VMEM((1, B_WIDTH), F32),
                        pltpu.VMEM((B_HEADS, 1, 1), F32)],
        compiler_params=_params(),
        name="mlstm",
    )(p_b, p_b, p_if, g_t, conv_w, conv_b, bif_row, bif_col, hn_g, ltri, utri)


def _swa_kernel(q_ref, kvc_ref, kvp_ref, qn_ref, kn_ref, sink_ref, o_ref, *, blocks_per_seq):
    i = pl.program_id(0)
    first = (i % blocks_per_seq) == 0
    bk = C_BLOCK
    hp = LANES
    group = C_Q_HEADS // C_KV_HEADS
    rows = group * bk
    kv = jnp.concatenate([kvp_ref[...], kvc_ref[...]], axis=0)
    qrow = lax.broadcasted_iota(jnp.int32, (rows, 2 * bk), 0)
    kj = lax.broadcasted_iota(jnp.int32, (rows, 2 * bk), 1)
    dist = qrow % bk + bk - kj
    valid = (dist >= 0) & (dist < WINDOW) & (kj >= jnp.where(first, bk, 0))
    distf = dist.astype(F32)
    rgroup = lax.broadcasted_iota(jnp.int32, (rows, 1), 0) // bk
    inv_d = 1.0 / C_HEAD_DIM
    qn = qn_ref[...]
    kn = kn_ref[...]

    def per_row_group(vals):
        out = vals[-1]
        for j in range(group - 2, -1, -1):
            out = jnp.where(rgroup == j, vals[j], out)
        return out

    for kvh in range(C_KV_HEADS):
        kh = kv[:, kvh * hp:(kvh + 1) * hp]
        kh = kh * lax.rsqrt(jnp.sum(kh * kh, axis=-1, keepdims=True) * inv_d + EPS) * kn
        vh = kv[:, (C_KV_HEADS + kvh) * hp:(C_KV_HEADS + kvh + 1) * hp]
        heads = [kvh * group + j for j in range(group)]
        q4 = jnp.concatenate([q_ref[:, h * hp:(h + 1) * hp] for h in heads], axis=0)
        q4 = q4 * lax.rsqrt(jnp.sum(q4 * q4, axis=-1, keepdims=True) * inv_d + EPS) * qn * (C_HEAD_DIM ** -0.5)
        slope = per_row_group([float(2.0 ** (-8.0 * (h + 1) / C_Q_HEADS)) for h in heads])
        sink = per_row_group([sink_ref[:, h:h + 1] for h in heads])
        s = jnp.where(valid, _mm_nt(q4, kh) - slope * distf, -jnp.inf)
        m = jnp.maximum(jnp.max(s, axis=-1, keepdims=True), sink)
        p = jnp.exp(s - m)
        den = jnp.sum(p, axis=-1, keepdims=True) + jnp.exp(sink - m)
        out = _mm(p, vh) * (1.0 / den)
        for pair in range(group // 2):
            even = out[(2 * pair) * bk:(2 * pair + 1) * bk]
            odd = out[(2 * pair + 1) * bk:(2 * pair + 2) * bk]
            t = kvh * (group // 2) + pair
            o_ref[:, t * hp:(t + 1) * hp] = even + pltpu.roll(odd, C_HEAD_DIM, 1)


def _swa(p_cq, p_ckv, qn_g, kn_g, sink_row, seq):
    nt = p_cq.shape[0]
    bk = C_BLOCK
    nb = seq // bk
    return pl.pallas_call(
        functools.partial(_swa_kernel, blocks_per_seq=nb),
        grid=(nt // bk,),
        in_specs=[pl.BlockSpec((bk, p_cq.shape[1]), lambda i: (i, 0)),
                  pl.BlockSpec((bk, p_ckv.shape[1]), lambda i: (i, 0)),
                  pl.BlockSpec((bk, p_ckv.shape[1]), lambda i: (jnp.maximum(i - 1, 0), 0)),
                  _full(qn_g.shape), _full(kn_g.shape), _full(sink_row.shape)],
        out_specs=pl.BlockSpec((bk, C_WIDTH), lambda i: (i, 0)),
        out_shape=jax.ShapeDtypeStruct((nt, C_WIDTH), F32),
        compiler_params=_params(),
        name="swa",
    )(p_cq, p_ckv, p_ckv, qn_g, kn_g, sink_row)


def _merge_kernel(x_ref, gt_ref, pg_ref, ya_ref, r_ref, k_ref, v_ref, g_ref, yb_ref, yc_ref,
                  hsum_ref, lng_ref, lnb_ref, rk_ref, wa_ref, wb_ref, wc_ref, wo_ref, o_ref):
    hsum = hsum_ref[...]
    inv_n = 1.0 / A_HEAD_DIM
    y = ya_ref[...]
    mu = _mm_f32_lhs(y, hsum) * inv_n
    dlt = y - mu
    var = _mm_f32_lhs(dlt * dlt, hsum) * inv_n
    yn = dlt * lax.rsqrt(var + A_LN_EPS) * lng_ref[...] + lnb_ref[...]
    bonus = _mm_f32_lhs(r_ref[...] * k_ref[...] * rk_ref[...], hsum) * v_ref[...]
    y_a = (yn + bonus) * g_ref[...]
    d = x_ref.shape[1]
    pg = pg_ref[...]
    merged = (jax.nn.sigmoid(pg[:, :d]) * _mm(y_a, wa_ref[...])
              + jax.nn.sigmoid(pg[:, d:2 * d]) * _mm(yb_ref[...], wb_ref[...])
              + jax.nn.sigmoid(pg[:, 2 * d:]) * _mm(yc_ref[...], wc_ref[...]))
    o_ref[...] = x_ref[...] + gt_ref[0] * _mm(merged, wo_ref[...])


def _merge(x2, gt, p_g, y_scan, r, k, v, g, y_b, y_c, hsum, lng, lnb, rk, wa, wb, wc, wo, seq):
    nt, d = x2.shape
    tm = ROW_TILE
    tps = seq // tm
    row = lambda w: pl.BlockSpec((tm, w), lambda i: (i, 0))
    consts = (hsum, lng, lnb, rk, wa, wb, wc, wo)
    return pl.pallas_call(
        _merge_kernel,
        grid=(nt // tm,),
        in_specs=[row(d), pl.BlockSpec((1, 1, d), lambda i: (i // tps, 0, 0)), row(3 * d)]
                 + [row(A_WIDTH)] * 5 + [row(B_WIDTH), row(C_WIDTH)] + [_full(c.shape) for c in consts],
        out_specs=row(d),
        out_shape=jax.ShapeDtypeStruct((nt, d), F32),
        compiler_params=_params(),
        name="merge",
    )(x2, gt, p_g, y_scan, r, k, v, g, y_b, y_c, *consts)


def _route_kernel(x_ref, g_ref, sc_ref, sh_ref, wr_ref, br_ref, lstrict_ref, h_ref, idx_ref, wgt_ref, cnt_ref):
    x = x_ref[...]
    y = x * lax.rsqrt(jnp.mean(x * x, axis=-1, keepdims=True) + EPS) * g_ref[...]
    h = y * (1.0 + sc_ref[0]) + sh_ref[0]
    _store_token_tiles(h_ref, h)
    lg = _mm3(h, wr_ref[...]) + br_ref[...]
    lane = lax.broadcasted_iota(jnp.int32, lg.shape, 1)
    lanef = lane.astype(F32)
    ninf = -jnp.inf
    big = float(LANES)

    def first_argmax(vals, vmax):
        return jnp.min(jnp.where(vals == vmax, lanef, big), axis=-1, keepdims=True)

    gl = jnp.where(lane < N_GROUPS, lg, ninf)
    gmax = jnp.max(gl, axis=-1, keepdims=True)
    g_top = 1.0 / jnp.sum(jnp.exp(gl - gmax), axis=-1, keepdims=True)
    g_idx = first_argmax(gl, gmax)
    lo = N_GROUPS + EXPERTS_PER_GROUP * g_idx
    el = jnp.where((lanef >= lo) & (lanef < lo + EXPERTS_PER_GROUP), lg, ninf)
    e1 = jnp.max(el, axis=-1, keepdims=True)
    z = jnp.sum(jnp.exp(el - e1), axis=-1, keepdims=True)
    i1 = first_argmax(el, e1)
    el2 = jnp.where(lanef == i1, ninf, el)
    e2 = jnp.max(el2, axis=-1, keepdims=True)
    i2 = first_argmax(el2, e2)
    p1 = 1.0 / z
    p2 = jnp.exp(e2 - e1) / z
    w1 = g_top * p1 / (p1 + p2)
    w2 = g_top * p2 / (p1 + p2)
    oh1 = jnp.where(lanef == i1, 1.0, 0.0)
    oh2 = jnp.where(lanef == i2, 1.0, 0.0)
    both = oh1 + oh2
    before = _mm(lstrict_ref[...], both)
    r1 = jnp.sum(before * oh1, axis=-1, keepdims=True)
    r2 = jnp.sum(before * oh2, axis=-1, keepdims=True)
    idx = jnp.where(lane == 0, i1 - N_GROUPS, jnp.where(lane == 1, i2 - N_GROUPS,
                    jnp.where(lane == 2, r1, jnp.where(lane == 3, r2, 0.0))))
    idx_ref[...] = idx.astype(jnp.int32)
    wgt_ref[...] = jnp.where(lane == 0, w1, jnp.where(lane == 1, w2, 0.0))
    counts = jnp.sum(both, axis=0, keepdims=True)
    cnt_ref[0] = jnp.broadcast_to(counts, cnt_ref.shape[1:]).astype(jnp.int32)


def _route(x2, g, sc, sh, w_r, b_r, seq):
    nt, d = x2.shape
    tm = ROW_TILE
    tps = seq // tm
    n_tiles = nt // tm
    row = lambda w: pl.BlockSpec((tm, w), lambda i: (i, 0))
    mod = pl.BlockSpec((1, 1, d), lambda i: (i // tps, 0, 0))
    lstrict = jnp.tril(jnp.ones((tm, tm), F32), k=-1)
    return pl.pallas_call(
        _route_kernel,
        grid=(n_tiles,),
        in_specs=[row(d), _full((1, d)), mod, mod, _full(w_r.shape), _full(b_r.shape), _full(lstrict.shape)],
        out_specs=[pl.BlockSpec((tm * TILE_SLABS, LANES), lambda i: (i, 0)), row(LANES), row(LANES),
                   pl.BlockSpec((1, SUBLANES, LANES), lambda i: (i, 0, 0))],
        out_shape=[jax.ShapeDtypeStruct((nt * TILE_SLABS, LANES), F32), jax.ShapeDtypeStruct((nt, LANES), jnp.int32),
                   jax.ShapeDtypeStruct((nt, LANES), F32),
                   jax.ShapeDtypeStruct((n_tiles, SUBLANES, LANES), jnp.int32)],
        compiler_params=_params(),
        name="route",
    )(x2, g, sc, sh, w_r, b_r, lstrict)


def _expert_kernel(bexp_ref, tok_ref, tok_next_ref, h_hbm, win_ref, wout_ref, o_ref, buf, sem, win_bf, wout_bf):
    i = pl.program_id(0)
    slot = _pipelined_row_gather(tok_ref, tok_next_ref, MOE_BLOCK, h_hbm, buf, sem)

    @pl.when(jnp.logical_or(i == 0, bexp_ref[i] != bexp_ref[jnp.maximum(i - 1, 0)]))
    def _():
        win_bf[...] = win_ref[0, 0].astype(BF16)
        wout_bf[...] = wout_ref[0, 0].astype(BF16)

    rows = buf.at[slot]
    gu = None
    for s in range(0, TILE_SLABS, 2):
        xs = jnp.concatenate([_slab(rows, 0, MOE_BLOCK, s), _slab(rows, 0, MOE_BLOCK, s + 1)], axis=1)
        part = _mm(xs, win_bf[s * LANES:(s + 2) * LANES, :])
        gu = part if gu is None else gu + part
    gate = gu[:, :D_EXPERT]
    up = gu[:, D_EXPERT:]
    _store_token_tiles(o_ref, _mm(gate * jax.nn.sigmoid(gate) * up, wout_bf[...]))


def _experts(block_exp, slot_row, h2, w_in_e, w_out_e, layer):
    n_blocks = block_exp.shape[0]
    rows = MOE_BLOCK
    d = TILE_SLABS * LANES
    tok3 = slot_row.reshape(n_blocks, 1, rows)
    smem_blk = lambda f: pl.BlockSpec((1, 1, rows), f, memory_space=pltpu.SMEM)
    grid_spec = pltpu.PrefetchScalarGridSpec(
        num_scalar_prefetch=1,
        grid=(n_blocks,),
        in_specs=[smem_blk(lambda i, be: (i, 0, 0)),
                  smem_blk(lambda i, be: (jnp.minimum(i + 1, n_blocks - 1), 0, 0)),
                  pl.BlockSpec(memory_space=pl.ANY),
                  pl.BlockSpec((1, 1, d, 2 * D_EXPERT), lambda i, be: (layer, be[i], 0, 0)),
                  pl.BlockSpec((1, 1, D_EXPERT, d), lambda i, be: (layer, be[i], 0, 0))],
        out_specs=pl.BlockSpec((rows * TILE_SLABS, LANES), lambda i, be: (i, 0)),
        scratch_shapes=[pltpu.VMEM((2, rows * TILE_SLABS, LANES), F32), pltpu.SemaphoreType.DMA((2,)),
                        pltpu.VMEM((d, 2 * D_EXPERT), BF16), pltpu.VMEM((D_EXPERT, d), BF16)],
    )
    return pl.pallas_call(
        _expert_kernel,
        grid_spec=grid_spec,
        out_shape=jax.ShapeDtypeStruct((n_blocks * rows * TILE_SLABS, LANES), F32),
        compiler_params=_params(),
        name="experts",
    )(block_exp, tok3, tok3, h2, w_in_e, w_out_e)


def _combine_kernel(pos_ref, pos_next_ref, x_ref, gt_ref, wgt_ref, yb_hbm, o_ref, buf, sem):
    o_ref[...] = _moe_residual(pos_ref, pos_next_ref, x_ref, gt_ref, wgt_ref, yb_hbm, buf, sem)


def _combine(pos, x2, gt, wgt, yb, seq):
    nt, d = x2.shape
    tm = ROW_TILE
    n_tiles = nt // tm
    pos3, specs, scratch = _moe_residual_specs(pos, tm, n_tiles, d, seq // tm)
    return pl.pallas_call(
        _combine_kernel,
        grid=(n_tiles,),
        in_specs=specs,
        out_specs=pl.BlockSpec((tm, d), lambda i: (i, 0)),
        out_shape=jax.ShapeDtypeStruct((nt, d), F32),
        scratch_shapes=scratch,
        compiler_params=_params(),
        name="combine",
    )(pos3, pos3, x2, gt, wgt, yb)


def _dispatch_plan(ridx, tile_counts, n_tokens):
    n_tiles = tile_counts.shape[0]
    tm = n_tokens // n_tiles
    na = n_tokens * TOP_K
    counts = tile_counts[:, 0, N_GROUPS:N_GROUPS + N_EXPERTS]
    tile_off = jnp.cumsum(counts, axis=0) - counts
    total = jnp.sum(counts, axis=0)
    padded = (total + MOE_BLOCK - 1) // MOE_BLOCK * MOE_BLOCK
    pad_ends = jnp.cumsum(padded)
    base = (pad_ends - padded)[None, :] + tile_off
    expert = ridx[:, :TOP_K].reshape(n_tiles, tm, TOP_K)
    rank = ridx[:, TOP_K:2 * TOP_K].reshape(n_tiles, tm, TOP_K)
    onehot = expert[..., None] == jnp.arange(N_EXPERTS, dtype=jnp.int32)
    dest = jnp.sum(jnp.where(onehot, base[:, None, None, :], 0), axis=-1) + rank
    dest = dest.reshape(n_tokens, TOP_K)
    n_blocks = -(-na // MOE_BLOCK) + N_EXPERTS
    slot_row = jnp.zeros((n_blocks * MOE_BLOCK,), jnp.int32).at[dest.reshape(-1)].set(
        jnp.arange(na, dtype=jnp.int32) // TOP_K * TILE_SLABS)
    block_start = jnp.arange(n_blocks, dtype=jnp.int32) * MOE_BLOCK
    block_exp = jnp.minimum(jnp.sum((pad_ends[None, :] <= block_start[:, None]).astype(jnp.int32), axis=1),
                            N_EXPERTS - 1)
    return block_exp, slot_row, dest * TILE_SLABS


def _pad_cols(w, width):
    return jnp.pad(w, ((0, 0), (0, width - w.shape[1])))


def _pad_heads(w, head_dim):
    rows = w.shape[0]
    w = w.reshape(rows, -1, head_dim)
    return jnp.pad(w, ((0, 0), (0, 0), (0, LANES - head_dim))).reshape(rows, -1)


def kernel(x, c, w_ada, b_ada, norm1_g, norm2_g, w_in, mu_a, w0_a, w2_a, a0_a, a2_a, g2_a, kk_a, ka_a, rk_a,
           lnx_g_a, lnx_b_a, conv_w_b, conv_b_b, bi_b, bf_b, hn_g_b, qn_g_c, kn_g_c, sink_c, w_br_a, w_br_b,
           w_br_c, w_out, w_rg, b_rg, w_re, b_re, w_e_in, w_e_out):
    bsz, seq, d = x.shape
    nt = bsz * seq
    depth = w_in.shape[0]
    assert bsz * A_HEADS * VAL_LANE_REPS == LANES, "the RWKV recurrence packs (batch, head, 4 value indices) on lanes"
    assert seq % ROW_TILE == 0 and seq % B_CHUNK == 0 and seq % SCAN_STEPS == 0

    mod = _ada_mod(c, w_ada, b_ada)
    hsum = jnp.kron(jnp.eye(A_HEADS, dtype=F32), jnp.ones((A_HEAD_DIM, A_HEAD_DIM), F32))
    tri = jnp.tril(jnp.ones((B_CHUNK, B_CHUNK), F32))

    o_b = 1024
    o_if = o_b + 4 * B_WIDTH
    o_cq = o_if + 2 * B_HEADS
    o_ckv = o_cq + C_WIDTH
    o_g = o_ckv + 2 * C_KV_WIDTH

    x2 = x.reshape(nt, d)
    moe = None
    for l in range(depth):
        m = mod[l].reshape(bsz, 6, 1, d)
        sh1, sc1, gt1, sh2, sc2, gt2 = (m[:, i] for i in range(6))
        wl = w_in[l]
        weights = [wl[:, o_g:].astype(BF16), wl[:, :o_b].astype(BF16), wl[:, o_b:o_if].astype(BF16),
                   _pad_heads(wl[:, o_cq:o_ckv], C_HEAD_DIM).astype(BF16),
                   _pad_heads(wl[:, o_ckv:o_g], C_HEAD_DIM).astype(BF16),
                   _pad_cols(wl[:, o_if:o_cq], LANES).astype(BF16)]
        outs = _norm_proj(x2, norm1_g[l][None], sc1, sh1, weights, seq, moe)
        if moe is not None:
            x2 = outs.pop(0)
        p_g, p_a, p_b, p_cq, p_ckv, p_if = outs

        wwa = (jnp.zeros((A_DECAY_LORA + A_ICL_LORA, 2 * A_WIDTH), F32)
               .at[:A_DECAY_LORA, :A_WIDTH].set(w2_a[l]).at[A_DECAY_LORA:, A_WIDTH:].set(a2_a[l]))
        r, dec, k2, v, kk, bvec, g = _rwkv_prep(p_a, mu_a[l][None], wwa, w0_a[l][None], a0_a[l][None], g2_a[l],
                                                kk_a[l][None], ka_a[l][None], hsum, seq)
        y_scan = _rwkv_scan(*(_key_layout(t, bsz, seq) for t in (dec, kk, bvec, k2, r)),
                            _value_layout(v, bsz, seq))
        y_scan = _value_unlayout(y_scan, bsz, seq)

        g_t = p_if[:, :SUBLANES].reshape(bsz, seq, SUBLANES).transpose(0, 2, 1)
        bif = jnp.concatenate([bi_b[l], bf_b[l]])
        y_b = _mlstm(p_b, p_if, g_t, conv_w_b[l], conv_b_b[l][None], _pad_cols(bif[None], LANES), bif[:, None],
                     hn_g_b[l].reshape(1, B_WIDTH), tri, tri.T, seq)

        y_c = _swa(p_cq, p_ckv, _pad_cols(qn_g_c[l][None], LANES), _pad_cols(kn_g_c[l][None], LANES),
                   _pad_cols(sink_c[l][None], LANES), seq)

        x2 = _merge(x2, gt1, p_g, y_scan, r, k2, v, g, y_b, y_c, hsum, lnx_g_a[l][None], lnx_b_a[l][None],
                    rk_a[l].reshape(1, A_WIDTH), w_br_a[l].astype(BF16), w_br_b[l].astype(BF16),
                    w_br_c[l].astype(BF16), w_out[l].astype(BF16), seq)

        w_r = _pad_cols(jnp.concatenate([w_rg[l], w_re[l]], axis=1), LANES)
        b_r = _pad_cols(jnp.concatenate([b_rg[l], b_re[l]])[None], LANES)
        h2, ridx, rwgt, tile_counts = _route(x2, norm2_g[l][None], sc2, sh2, w_r, b_r, seq)
        block_exp, slot_row, pos = _dispatch_plan(ridx, tile_counts, nt)
        yb = _experts(block_exp, slot_row, h2, w_e_in, w_e_out, l)
        moe = (pos, gt2, rwgt, yb)
    pos, gt2, rwgt, yb = moe
    x2 = _combine(pos, x2, gt2, rwgt, yb, seq)
    return x2.reshape(bsz, seq, d)
```

```python
import functools

import numpy as np
import jax
import jax.numpy as jnp
from jax import lax
from jax.experimental import pallas as pl
from jax.experimental.pallas import tpu as pltpu

F32 = jnp.float32
BF16 = jnp.bfloat16

A_HEADS = 4
A_HEAD_DIM = 64
A_WIDTH = 256
A_DECAY_LORA = 64
A_ICL_LORA = 64
A_LN_EPS = 64e-5
B_HEADS = 4
B_HEAD_DIM = 64
B_WIDTH = 256
B_CHUNK = 128
C_Q_HEADS = 8
C_KV_HEADS = 2
C_HEAD_DIM = 64
C_WIDTH = 512
C_KV_WIDTH = 128
WINDOW = 128
C_BLOCK = 128
N_GROUPS = 4
EXPERTS_PER_GROUP = 8
N_EXPERTS = 32
TOP_K = 2
D_EXPERT = 512
MOE_BLOCK = 128
EPS = 1e-6

LANES = 128
SUBLANES = 8
VMEM_LIMIT = 56 * 1024 * 1024

ROW_TILE = 256
SCAN_STEPS = 32
ADA_COL_TILE = 1536
SWA_STACK = 4


def _mm(a, b):
    return jnp.dot(a.astype(BF16), b.astype(BF16), preferred_element_type=F32)


def _mm32(a, b):
    return jnp.dot(a, b, precision=lax.Precision.HIGHEST, preferred_element_type=F32)


def _bf16_terms(x, n):
    terms = []
    for t in range(n):
        p = x.astype(BF16)
        terms.append(p)
        if t + 1 < n:
            x = x - p.astype(F32)
    return terms


def _sum_dots(pairs):
    out = None
    for a, b in reversed(pairs):
        t = jnp.dot(a, b, preferred_element_type=F32)
        out = t if out is None else out + t
    return out


def _mm_f32_lhs(a, b01):
    b = b01.astype(BF16)
    return _sum_dots([(p, b) for p in _bf16_terms(a, 3)])


def _mm3(a, b):
    a_hi, a_lo = _bf16_terms(a, 2)
    b_hi, b_lo = _bf16_terms(b, 2)
    return _sum_dots([(a_hi, b_hi), (a_hi, b_lo), (a_lo, b_hi)])


def _mm_nt(a, b):
    return lax.dot_general(a.astype(BF16), b.astype(BF16), (((1,), (1,)), ((), ())),
                           preferred_element_type=F32)


def _mm_tn(a, b):
    return lax.dot_general(a.astype(BF16), b.astype(BF16), (((0,), (0,)), ((), ())),
                           preferred_element_type=F32)


def _log_sigmoid(x):
    return jnp.minimum(x, 0.0) - jnp.log(1.0 + jnp.exp(-jnp.abs(x)))


def _params(n_axes=1):
    return pltpu.CompilerParams(dimension_semantics=("arbitrary",) * n_axes,
                                vmem_limit_bytes=VMEM_LIMIT)


def _full(shape):
    nd = len(shape)
    return pl.BlockSpec(shape, lambda *_: (0,) * nd)


def _ada_kernel(c_ref, w_ref, b_ref, o_ref):
    c = c_ref[...]
    cond = c * jax.nn.sigmoid(c)
    o_ref[0] = _mm(cond, w_ref[0]) + b_ref[0]


def _ada_mod(c, w_ada, b_ada):
    depth, d, n = w_ada.shape
    bsz = c.shape[0]
    tn = ADA_COL_TILE
    return pl.pallas_call(
        _ada_kernel,
        grid=(depth, n // tn),
        in_specs=[pl.BlockSpec((bsz, d), lambda l, j: (0, 0)),
                  pl.BlockSpec((1, d, tn), lambda l, j: (l, 0, j)),
                  pl.BlockSpec((1, 1, tn), lambda l, j: (l, 0, j))],
        out_specs=pl.BlockSpec((1, bsz, tn), lambda l, j: (l, 0, j)),
        out_shape=jax.ShapeDtypeStruct((depth, bsz, n), F32),
        compiler_params=_params(2),
        name="ada_mod",
    )(c, w_ada, b_ada.reshape(depth, 1, n))


TILE_SLABS = SUBLANES


def _slab(ref, first_token, n_tokens, s):
    return ref[pl.ds(first_token * TILE_SLABS + s, n_tokens, stride=TILE_SLABS), :]


def _store_token_tiles(ref, x):
    n = x.shape[0]
    for s in range(TILE_SLABS):
        ref[pl.ds(s, n, stride=TILE_SLABS), :] = x[:, s * LANES:(s + 1) * LANES]


def _start_row_gather(idx_ref, n_rows, src_hbm, dst, sem):
    for r in range(n_rows):
        first = pl.multiple_of(idx_ref[0, 0, r], TILE_SLABS)
        pltpu.make_async_copy(src_hbm.at[pl.ds(first, TILE_SLABS), :], dst.at[pl.ds(r * TILE_SLABS, TILE_SLABS), :],
                              sem).start(priority=r % 2)


def _wait_row_gather(n_rows, src_hbm, dst, sem):
    pltpu.make_async_copy(src_hbm.at[pl.ds(0, n_rows * TILE_SLABS), :], dst, sem).wait()


def _pipelined_row_gather(idx_ref, idx_next_ref, n_rows, src_hbm, buf, sem):
    i = pl.program_id(0)
    slot = i % 2

    @pl.when(i == 0)
    def _():
        _start_row_gather(idx_ref, n_rows, src_hbm, buf.at[0], sem.at[0])

    _start_row_gather(idx_next_ref, n_rows, src_hbm, buf.at[1 - slot], sem.at[1 - slot])
    _wait_row_gather(n_rows, src_hbm, buf.at[slot], sem.at[slot])

    @pl.when(i == pl.num_programs(0) - 1)
    def _():
        _wait_row_gather(n_rows, src_hbm, buf.at[1 - slot], sem.at[1 - slot])

    return slot


def _moe_residual(pos_ref, pos_next_ref, x_ref, gt_ref, wgt_ref, yb_hbm, buf, sem):
    tm = x_ref.shape[0]
    slot = _pipelined_row_gather(pos_ref, pos_next_ref, TOP_K * tm, yb_hbm, buf, sem)
    wgt = wgt_ref[...]
    w1, w2 = wgt[:, 0:1], wgt[:, 1:2]
    rows = buf.at[slot]
    gt = gt_ref[0]
    slabs = []
    for s in range(TILE_SLABS):
        lanes = slice(s * LANES, (s + 1) * LANES)
        y = _slab(rows, 0, tm, s) * w1 + _slab(rows, tm, tm, s) * w2
        slabs.append(x_ref[:, lanes] + gt[:, lanes] * y)
    return jnp.concatenate(slabs, axis=1)


def _moe_residual_specs(pos, tm, n_tiles, d, tps):
    pos3 = pos.reshape(n_tiles, tm, TOP_K).transpose(0, 2, 1).reshape(n_tiles, 1, TOP_K * tm)
    smem_blk = lambda f: pl.BlockSpec((1, 1, TOP_K * tm), f, memory_space=pltpu.SMEM)
    row = lambda w: pl.BlockSpec((tm, w), lambda i: (i, 0))
    specs = [smem_blk(lambda i: (i, 0, 0)),
             smem_blk(lambda i: (jnp.minimum(i + 1, n_tiles - 1), 0, 0)),
             row(d), pl.BlockSpec((1, 1, d), lambda i: (i // tps, 0, 0)), row(LANES),
             pl.BlockSpec(memory_space=pl.ANY)]
    assert d == TILE_SLABS * LANES
    scratch = [pltpu.VMEM((2, TOP_K * tm * TILE_SLABS, LANES), F32), pltpu.SemaphoreType.DMA((2,))]
    return pos3, specs, scratch


def _norm_proj_kernel(*refs, n_proj, moe):
    if moe:
        x = _moe_residual(*refs[:6], *refs[-2:])
        refs = refs[6:-2]
        refs[-1][...] = x
        refs = refs[:-1]
    else:
        x = refs[0][...]
        refs = refs[1:]
    g_ref, sc_ref, sh_ref = refs[:3]
    y = x * lax.rsqrt(jnp.mean(x * x, axis=-1, keepdims=True) + EPS) * g_ref[...]
    h = (y * (1.0 + sc_ref[0]) + sh_ref[0]).astype(BF16)
    for w_ref, o_ref in zip(refs[3:3 + n_proj], refs[3 + n_proj:]):
        o_ref[...] = jnp.dot(h, w_ref[...], preferred_element_type=F32).astype(o_ref.dtype)


def _norm_proj(x2, g, sc, sh, weights, out_dtypes, seq, moe=None):
    nt, d = x2.shape
    tm = ROW_TILE
    tps = seq // tm
    n_tiles = nt // tm
    row = lambda w: pl.BlockSpec((tm, w), lambda i: (i, 0))
    mod = pl.BlockSpec((1, 1, d), lambda i: (i // tps, 0, 0))
    in_specs = [_full((1, d)), mod, mod] + [_full(w.shape) for w in weights]
    out_specs = [row(w.shape[1]) for w in weights]
    out_shape = [jax.ShapeDtypeStruct((nt, w.shape[1]), dt) for w, dt in zip(weights, out_dtypes)]
    if moe is None:
        args, head_specs, scratch = [x2], [row(d)], []
    else:
        pos, gt, wgt, yb = moe
        pos3, head_specs, scratch = _moe_residual_specs(pos, tm, n_tiles, d, tps)
        args = [pos3, pos3, x2, gt, wgt, yb]
        out_specs.append(row(d))
        out_shape.append(jax.ShapeDtypeStruct((nt, d), F32))
    outs = pl.pallas_call(
        functools.partial(_norm_proj_kernel, n_proj=len(weights), moe=moe is not None),
        grid=(n_tiles,),
        in_specs=head_specs + in_specs,
        out_specs=out_specs,
        out_shape=out_shape,
        scratch_shapes=scratch,
        compiler_params=_params(),
        name="norm_proj_moe" if moe is not None else "norm_proj",
    )(*args, g, sc, sh, *weights)
    outs = list(outs)
    return outs if moe is None else [outs[-1]] + outs[:-1]


def _rwkv_prep_kernel(p_ref, prev_ref, mu_ref, wwa_ref, w0_ref, a0_ref, g2_ref, kkw_ref, kaw_ref, hsum_ref,
                      r_out, w_out, k_out, v_out, kk_out, b_out, g_out, *, tiles_per_seq):
    i = pl.program_id(0)
    p = p_ref[...]
    first = (i % tiles_per_seq) == 0
    prev_row = jnp.where(first, 0.0, prev_ref[SUBLANES - 1:SUBLANES, :])
    row = lax.broadcasted_iota(jnp.int32, p.shape, 0)
    shifted = jnp.where(row == 0, prev_row, pltpu.roll(p, 1, 0))
    pa = p + mu_ref[...] * (shifted - p)
    aw = A_WIDTH
    o_lora = 3 * aw
    o_glo = o_lora + A_DECAY_LORA + A_ICL_LORA
    r = pa[:, 0:aw]
    k = pa[:, aw:2 * aw]
    v = pa[:, 2 * aw:o_lora]
    slab = pa[:, o_lora:o_glo]
    glo = pa[:, o_glo:]
    lane = lax.broadcasted_iota(jnp.int32, slab.shape, 1)
    z = jnp.where(lane < A_DECAY_LORA, jnp.tanh(slab), slab)
    wa = _mm3(z, wwa_ref[...])
    w_log = _log_sigmoid(w0_ref[...] + wa[:, :aw]) - 0.5
    decay = jnp.exp(-jnp.exp(w_log))
    a = jax.nn.sigmoid(a0_ref[...] + wa[:, aw:])
    g = _mm3(jax.nn.sigmoid(glo), g2_ref[...])
    kk = k * kkw_ref[...]
    ss = _mm_f32_lhs(kk * kk, hsum_ref[...])
    kk = kk / jnp.maximum(jnp.sqrt(ss), 1e-12)
    r_out[...] = r
    w_out[...] = decay
    k_out[...] = k * (1.0 + (a - 1.0) * kaw_ref[...])
    v_out[...] = v
    kk_out[...] = kk
    b_out[...] = kk * a
    g_out[...] = g


def _time_major_spec(tm, width, tiles_per_seq):
    return pl.BlockSpec((tm, width), lambda i: (i % tiles_per_seq, i // tiles_per_seq))


def _rwkv_prep(p_a, mu, wwa, w0, a0, g2, kkw, kaw, hsum, seq):
    nt, wd = p_a.shape
    tm = min(ROW_TILE, seq)
    tps = seq // tm
    row = lambda w: pl.BlockSpec((tm, w), lambda i: (i, 0))
    prev = pl.BlockSpec((SUBLANES, wd), lambda i: (jnp.maximum(i * (tm // SUBLANES) - 1, 0), 0))
    bsz = nt // seq
    outs = [jax.ShapeDtypeStruct((seq, bsz * A_WIDTH), F32)] * 7
    return pl.pallas_call(
        functools.partial(_rwkv_prep_kernel, tiles_per_seq=tps),
        grid=(nt // tm,),
        in_specs=[row(wd), prev, _full(mu.shape), _full(wwa.shape), _full(w0.shape), _full(a0.shape),
                  _full(g2.shape), _full(kkw.shape), _full(kaw.shape), _full(hsum.shape)],
        out_specs=[_time_major_spec(tm, A_WIDTH, tps)] * 7,
        out_shape=outs,
        compiler_params=_params(),
        name="rwkv_prep",
    )(p_a, p_a, mu, wwa, w0, a0, g2, kkw, kaw, hsum)


VAL_LANE_REPS = 4
VAL_TILES = A_HEAD_DIM // (SUBLANES * VAL_LANE_REPS)
SUM_CHAINS = 4


def _rwkv_scan_kernel(w_ref, kk_ref, b_ref, k_ref, r_ref, v_ref, y_ref, s_ref, *, steps):
    @pl.when(pl.program_id(0) == 0)
    def _():
        s_ref[...] = jnp.zeros_like(s_ref)

    def tree(terms):
        while len(terms) > 1:
            terms = [terms[i] + terms[i + 1] for i in range(0, len(terms), 2)]
        return terms[0]

    def step(t, carry):
        def row(ref, j):
            return jnp.broadcast_to(ref[t, j:j + 1, :], (SUBLANES, LANES))

        tiles = range(VAL_TILES)
        v_t = [v_ref[t, ih * SUBLANES:(ih + 1) * SUBLANES, :] for ih in tiles]
        acc = [[None] * SUM_CHAINS for _ in tiles]
        for j in range(A_HEAD_DIM):
            kk = row(kk_ref, j)
            for ih in tiles:
                term = s_ref[ih * A_HEAD_DIM + j] * kk
                c = j % SUM_CHAINS
                acc[ih][c] = term if acc[ih][c] is None else acc[ih][c] + term
        sa = [tree(a) for a in acc]
        y = [[None] * SUM_CHAINS for _ in tiles]
        for j in range(A_HEAD_DIM):
            w, b, k, r = (row(ref, j) for ref in (w_ref, b_ref, k_ref, r_ref))
            for ih in tiles:
                idx = ih * A_HEAD_DIM + j
                s = s_ref[idx] * w - sa[ih] * b + v_t[ih] * k
                s_ref[idx] = s
                term = s * r
                c = j % SUM_CHAINS
                y[ih][c] = term if y[ih][c] is None else y[ih][c] + term
        for ih in tiles:
            y_ref[t, ih * SUBLANES:(ih + 1) * SUBLANES, :] = tree(y[ih])
        return carry

    lax.fori_loop(0, steps, step, 0)


def _rwkv_scan(w_e, kk_e, b_e, k_e, r_e, v_p):
    seq = w_e.shape[0]
    tc = SCAN_STEPS
    key = pl.BlockSpec((tc, A_HEAD_DIM, LANES), lambda i: (i, 0, 0))
    val = pl.BlockSpec((tc, VAL_TILES * SUBLANES, LANES), lambda i: (i, 0, 0))
    return pl.pallas_call(
        functools.partial(_rwkv_scan_kernel, steps=tc),
        grid=(seq // tc,),
        in_specs=[key] * 5 + [val],
        out_specs=val,
        out_shape=jax.ShapeDtypeStruct((seq, VAL_TILES * SUBLANES, LANES), F32),
        scratch_shapes=[pltpu.VMEM((VAL_TILES * A_HEAD_DIM, SUBLANES, LANES), F32)],
        compiler_params=_params(),
        name="rwkv_scan",
    )(w_e, kk_e, b_e, k_e, r_e, v_p)


def _key_layout(x, bsz, seq):
    x = x.reshape(seq, bsz, A_HEADS, A_HEAD_DIM).transpose(0, 3, 1, 2)
    x = jnp.broadcast_to(x[..., None], (seq, A_HEAD_DIM, bsz, A_HEADS, VAL_LANE_REPS))
    return x.reshape(seq, A_HEAD_DIM, LANES)


def _value_layout(x, bsz, seq):
    x = x.reshape(seq, bsz, A_HEADS, VAL_TILES, SUBLANES, VAL_LANE_REPS).transpose(0, 3, 4, 1, 2, 5)
    return x.reshape(seq, VAL_TILES * SUBLANES, LANES)


def _value_unlayout(y, bsz, seq):
    y = y.reshape(seq, VAL_TILES, SUBLANES, bsz, A_HEADS, VAL_LANE_REPS).transpose(0, 3, 4, 1, 2, 5)
    return y.reshape(seq, bsz * A_WIDTH)


def _mlstm_kernel(pb_ref, prev_ref, pif_ref, gt_ref, cw_ref, cb_ref, bifr_ref, bifc_ref, hng_ref,
                  ltri_ref, utri_ref, o_ref, c_ref, n_ref, m_ref, *, chunks_per_seq):
    i = pl.program_id(0)
    first = (i % chunks_per_seq) == 0

    @pl.when(first)
    def _():
        c_ref[...] = jnp.zeros_like(c_ref)
        n_ref[...] = jnp.zeros_like(n_ref)
        m_ref[...] = jnp.zeros_like(m_ref)

    L = B_CHUNK
    pb = pb_ref[...]
    x = pb[:, :2 * B_WIDTH]
    prev = jnp.where(first, 0.0, prev_ref[:, :2 * B_WIDTH])
    cw = cw_ref[...]
    taps = cw.shape[0]
    acc = x * cw[taps - 1:taps, :] + cb_ref[...]
    r8 = lax.broadcasted_iota(jnp.int32, prev.shape, 0)
    for s in range(1, taps):
        xs = pltpu.roll(x, s, 0)
        head = jnp.where(r8 < s, pltpu.roll(prev, s, 0), xs[:SUBLANES])
        xs = jnp.concatenate([head, xs[SUBLANES:]], axis=0)
        acc = acc + xs * cw[taps - 1 - s:taps - s, :]
    qk = acc * jax.nn.sigmoid(acc)
    q = qk[:, :B_WIDTH]
    k = qk[:, B_WIDTH:] * (B_HEAD_DIM ** -0.5)
    v = pb[:, 2 * B_WIDTH:3 * B_WIDTH]
    o = pb[:, 3 * B_WIDTH:]

    pif = pif_ref[...] + bifr_ref[...]
    lane = lax.broadcasted_iota(jnp.int32, pif.shape, 1)
    logf_c = jnp.where(lane >= B_HEADS, _log_sigmoid(pif), 0.0)
    f_c = _mm32(ltri_ref[...], logf_c)
    g = gt_ref[0] + bifc_ref[...]
    row = lax.broadcasted_iota(jnp.int32, g.shape, 0)
    logf_r = jnp.where(row >= B_HEADS, _log_sigmoid(g), 0.0)
    f_r = _mm32(logf_r, utri_ref[...])

    ti = lax.broadcasted_iota(jnp.int32, (L, L), 0)
    si = lax.broadcasted_iota(jnp.int32, (L, L), 1)
    causal = si <= ti

    W = B_WIDTH
    lane_head = lax.broadcasted_iota(jnp.int32, (L, W), 1) // B_HEAD_DIM
    lane_head_row = lax.broadcasted_iota(jnp.int32, (1, W), 1) // B_HEAD_DIM
    row_head_col = lax.broadcasted_iota(jnp.int32, (W, 1), 0) // B_HEAD_DIM
    same_head = (lax.broadcasted_iota(jnp.int32, (W, W), 0) // B_HEAD_DIM
                 == lax.broadcasted_iota(jnp.int32, (W, W), 1) // B_HEAD_DIM)
    c0 = c_ref[...]
    n0 = n_ref[...]
    q_c = _mm_nt(q, c0)
    q_n = q * n0
    inv_d = 1.0 / B_HEAD_DIM

    hh = jnp.zeros((L, W), F32)
    w_all = jnp.zeros((L, W), F32)
    cd_lane = jnp.zeros((1, W), F32)
    cd_rows = jnp.zeros((W, 1), F32)
    for h in range(B_HEADS):
        hm = lane_head == h
        fc = f_c[:, B_HEADS + h:B_HEADS + h + 1]
        igc = pif[:, h:h + 1]
        fr = f_r[B_HEADS + h:B_HEADS + h + 1, :]
        igr = g[h:h + 1, :]
        f_last = fr[:, L - 1:L]
        m0 = m_ref[h]

        d = jnp.where(causal, fc - fr + igr, -jnp.inf)
        g_inter = fc + m0
        m_t = jnp.maximum(g_inter, jnp.max(d, axis=-1, keepdims=True))
        sw = _mm_nt(jnp.where(hm, q, 0.0), k) * jnp.exp(d - m_t)
        inter = jnp.exp(g_inter - m_t)
        den = (jnp.sum(sw, axis=-1, keepdims=True)
               + inter * jnp.sum(jnp.where(hm, q_n, 0.0), axis=-1, keepdims=True))
        inv = 1.0 / jnp.maximum(jnp.abs(den), jnp.exp(-m_t))
        hh = jnp.where(hm, (_mm(sw, v) + inter * q_c) * inv, hh)

        g_end_r = f_last - fr + igr
        m_new = jnp.maximum(f_last + m0, jnp.max(g_end_r, axis=-1, keepdims=True))
        carry_decay = jnp.exp(f_last + m0 - m_new)
        w_all = jnp.where(hm, jnp.exp(f_last - fc + igc - m_new), w_all)
        cd_lane = jnp.where(lane_head_row == h, carry_decay, cd_lane)
        cd_rows = jnp.where(row_head_col == h, carry_decay, cd_rows)
        m_ref[h] = m_new

    sq = hh * hh
    ms = jnp.zeros((L, W), F32)
    for h in range(B_HEADS):
        hm = lane_head == h
        ms = jnp.where(hm, jnp.sum(jnp.where(hm, sq, 0.0), axis=-1, keepdims=True) * inv_d, ms)
    o_ref[...] = hh * lax.rsqrt(ms + EPS) * hng_ref[...] * jax.nn.sigmoid(o)

    c_ref[...] = cd_rows * c0 + jnp.where(same_head, _mm_tn(v * w_all, k), 0.0)
    n_ref[...] = cd_lane * n0 + jnp.sum(k * w_all, axis=0, keepdims=True)


def _mlstm(p_b, p_if, g_t, conv_w, conv_b, bif_row, bif_col, hn_g, ltri, utri, seq):
    nt = p_b.shape[0]
    L = B_CHUNK
    nc = seq // L
    return pl.pallas_call(
        functools.partial(_mlstm_kernel, chunks_per_seq=nc),
        grid=(nt // L,),
        in_specs=[pl.BlockSpec((L, 4 * B_WIDTH), lambda i: (i, 0)),
                  pl.BlockSpec((SUBLANES, 4 * B_WIDTH), lambda i: (jnp.maximum(i * (L // SUBLANES) - 1, 0), 0)),
                  pl.BlockSpec((L, LANES), lambda i: (i, 0)),
                  pl.BlockSpec((1, SUBLANES, L), lambda i: (i // nc, 0, i % nc)),
                  _full(conv_w.shape), _full(conv_b.shape), _full(bif_row.shape), _full(bif_col.shape),
                  _full(hn_g.shape), _full(ltri.shape), _full(utri.shape)],
        out_specs=pl.BlockSpec((L, B_WIDTH), lambda i: (i, 0)),
        out_shape=jax.ShapeDtypeStruct((nt, B_WIDTH), F32),
        scratch_shapes=[pltpu.VMEM((B_WIDTH, B_WIDTH), F32),
                        pltpu.VMEM((1, B_WIDTH), F32),
                        pltpu.VMEM((B_HEADS, 1, 1), F32)],
        compiler_params=_params(),
        name="mlstm",
    )(p_b, p_b, p_if, g_t, conv_w, conv_b, bif_row, bif_col, hn_g, ltri, utri)


def _swa_kernel(q_ref, kvc_ref, kvp_ref, qn_ref, kn_ref, sink_ref, o_ref, *, blocks_per_seq):
    i = pl.program_id(0)
    first = (i % blocks_per_seq) == 0
    bk = C_BLOCK
    hp = LANES
    group = SWA_STACK
    rows = group * bk
    kv = jnp.concatenate([kvp_ref[...], kvc_ref[...]], axis=0)
    qrow = lax.broadcasted_iota(jnp.int32, (rows, 2 * bk), 0)
    kj = lax.broadcasted_iota(jnp.int32, (rows, 2 * bk), 1)
    dist = qrow % bk + bk - kj
    valid = (dist >= 0) & (dist < WINDOW) & (kj >= jnp.where(first, bk, 0))
    distf = dist.astype(F32)
    rgroup = lax.broadcasted_iota(jnp.int32, (rows, 1), 0) // bk
    inv_d = 1.0 / C_HEAD_DIM
    qn = qn_ref[...]
    kn = kn_ref[...]

    def per_row_group(vals):
        out = vals[-1]
        for j in range(group - 2, -1, -1):
            out = jnp.where(rgroup == j, vals[j], out)
        return out

    ones_head = jnp.ones((hp, hp), F32)
    ones_keys = jnp.ones((2 * bk, hp), BF16)

    heads_per_kv = C_Q_HEADS // C_KV_HEADS
    kn_heads, v_heads = [], []
    for kvh in range(C_KV_HEADS):
        kh = kv[:, kvh * hp:(kvh + 1) * hp]
        kn_heads.append((kh * lax.rsqrt(_mm_f32_lhs(kh * kh, ones_head) * inv_d + EPS) * kn).astype(BF16))
        v_heads.append(kv[:, (C_KV_HEADS + kvh) * hp:(C_KV_HEADS + kvh + 1) * hp].astype(BF16))

    for first_head in range(0, C_Q_HEADS, group):
        heads = list(range(first_head, first_head + group))
        kh = kn_heads[first_head // heads_per_kv]
        vh = v_heads[first_head // heads_per_kv]
        q4 = jnp.concatenate([q_ref[:, h * hp:(h + 1) * hp] for h in heads], axis=0)
        q4 = q4 * lax.rsqrt(_mm_f32_lhs(q4 * q4, ones_head) * inv_d + EPS) * qn * (C_HEAD_DIM ** -0.5)
        slope = per_row_group([float(2.0 ** (-8.0 * (h + 1) / C_Q_HEADS)) for h in heads])
        sink = per_row_group([sink_ref[:, h:h + 1] for h in heads])
        s = jnp.where(valid, _mm_nt(q4, kh) - slope * distf, -jnp.inf)
        m = jnp.maximum(jnp.max(s, axis=-1, keepdims=True), sink)
        p = jnp.exp(s - m).astype(BF16)
        den = jnp.dot(p, ones_keys, preferred_element_type=F32) + jnp.exp(sink - m)
        out = jnp.dot(p, vh, preferred_element_type=F32) * (1.0 / den)
        for pair in range(group // 2):
            even = out[(2 * pair) * bk:(2 * pair + 1) * bk]
            odd = out[(2 * pair + 1) * bk:(2 * pair + 2) * bk]
            t = first_head // 2 + pair
            o_ref[:, t * hp:(t + 1) * hp] = even + pltpu.roll(odd, C_HEAD_DIM, 1)


def _swa(p_cq, p_ckv, qn_g, kn_g, sink_row, seq):
    nt = p_cq.shape[0]
    bk = C_BLOCK
    nb = seq // bk
    return pl.pallas_call(
        functools.partial(_swa_kernel, blocks_per_seq=nb),
        grid=(nt // bk,),
        in_specs=[pl.BlockSpec((bk, p_cq.shape[1]), lambda i: (i, 0)),
                  pl.BlockSpec((bk, p_ckv.shape[1]), lambda i: (i, 0)),
                  pl.BlockSpec((bk, p_ckv.shape[1]), lambda i: (jnp.maximum(i - 1, 0), 0)),
                  _full(qn_g.shape), _full(kn_g.shape), _full(sink_row.shape)],
        out_specs=pl.BlockSpec((bk, C_WIDTH), lambda i: (i, 0)),
        out_shape=jax.ShapeDtypeStruct((nt, C_WIDTH), F32),
        compiler_params=_params(),
        name="swa",
    )(p_cq, p_ckv, p_ckv, qn_g, kn_g, sink_row)


def _merge_kernel(x_ref, gt_ref, pg_ref, ya_ref, r_ref, k_ref, v_ref, g_ref, yb_ref, yc_ref,
                  hsum_ref, lng_ref, lnb_ref, rk_ref, wa_ref, wb_ref, wc_ref, wo_ref, o_ref):
    hsum = hsum_ref[...]
    inv_n = 1.0 / A_HEAD_DIM
    y = ya_ref[...]
    mu = _mm_f32_lhs(y, hsum) * inv_n
    dlt = y - mu
    var = _mm_f32_lhs(dlt * dlt, hsum) * inv_n
    yn = dlt * lax.rsqrt(var + A_LN_EPS) * lng_ref[...] + lnb_ref[...]
    bonus = _mm_f32_lhs(r_ref[...] * k_ref[...] * rk_ref[...], hsum) * v_ref[...]
    y_a = (yn + bonus) * g_ref[...]
    d = x_ref.shape[1]
    pg = pg_ref[...].astype(F32)
    merged = (jax.nn.sigmoid(pg[:, :d]) * _mm(y_a, wa_ref[...])
              + jax.nn.sigmoid(pg[:, d:2 * d]) * _mm(yb_ref[...], wb_ref[...])
              + jax.nn.sigmoid(pg[:, 2 * d:]) * _mm(yc_ref[...], wc_ref[...]))
    o_ref[...] = x_ref[...] + gt_ref[0] * _mm(merged, wo_ref[...])


def _merge(x2, gt, p_g, y_scan, r, k, v, g, y_b, y_c, hsum, lng, lnb, rk, wa, wb, wc, wo, seq):
    nt, d = x2.shape
    tm = ROW_TILE
    tps = seq // tm
    row = lambda w: pl.BlockSpec((tm, w), lambda i: (i, 0))
    consts = (hsum, lng, lnb, rk, wa, wb, wc, wo)
    return pl.pallas_call(
        _merge_kernel,
        grid=(nt // tm,),
        in_specs=[row(d), pl.BlockSpec((1, 1, d), lambda i: (i // tps, 0, 0)), row(3 * d)]
                 + [_time_major_spec(tm, A_WIDTH, tps)] * 5 + [row(B_WIDTH), row(C_WIDTH)]
                 + [_full(c.shape) for c in consts],
        out_specs=row(d),
        out_shape=jax.ShapeDtypeStruct((nt, d), F32),
        compiler_params=_params(),
        name="merge",
    )(x2, gt, p_g, y_scan, r, k, v, g, y_b, y_c, *consts)


def _route_kernel(x_ref, g_ref, sc_ref, sh_ref, wr_ref, br_ref, lstrict_ref, h_ref, idx_ref, wgt_ref, cnt_ref):
    x = x_ref[...]
    y = x * lax.rsqrt(jnp.mean(x * x, axis=-1, keepdims=True) + EPS) * g_ref[...]
    h = y * (1.0 + sc_ref[0]) + sh_ref[0]
    _store_token_tiles(h_ref, h)
    lg = _mm3(h, wr_ref[...]) + br_ref[...]
    lane = lax.broadcasted_iota(jnp.int32, lg.shape, 1)
    lanef = lane.astype(F32)
    ninf = -jnp.inf
    big = float(LANES)

    def first_argmax(vals, vmax):
        return jnp.min(jnp.where(vals == vmax, lanef, big), axis=-1, keepdims=True)

    gl = jnp.where(lane < N_GROUPS, lg, ninf)
    gmax = jnp.max(gl, axis=-1, keepdims=True)
    g_top = 1.0 / jnp.sum(jnp.exp(gl - gmax), axis=-1, keepdims=True)
    g_idx = first_argmax(gl, gmax)
    lo = N_GROUPS + EXPERTS_PER_GROUP * g_idx
    el = jnp.where((lanef >= lo) & (lanef < lo + EXPERTS_PER_GROUP), lg, ninf)
    e1 = jnp.max(el, axis=-1, keepdims=True)
    z = jnp.sum(jnp.exp(el - e1), axis=-1, keepdims=True)
    i1 = first_argmax(el, e1)
    el2 = jnp.where(lanef == i1, ninf, el)
    e2 = jnp.max(el2, axis=-1, keepdims=True)
    i2 = first_argmax(el2, e2)
    p1 = 1.0 / z
    p2 = jnp.exp(e2 - e1) / z
    w1 = g_top * p1 / (p1 + p2)
    w2 = g_top * p2 / (p1 + p2)
    oh1 = jnp.where(lanef == i1, 1.0, 0.0)
    oh2 = jnp.where(lanef == i2, 1.0, 0.0)
    both = oh1 + oh2
    before = _mm(lstrict_ref[...], both)
    r1 = jnp.sum(before * oh1, axis=-1, keepdims=True)
    r2 = jnp.sum(before * oh2, axis=-1, keepdims=True)
    idx = jnp.where(lane == 0, i1 - N_GROUPS, jnp.where(lane == 1, i2 - N_GROUPS,
                    jnp.where(lane == 2, r1, jnp.where(lane == 3, r2, 0.0))))
    idx_ref[...] = idx.astype(jnp.int32)
    wgt_ref[...] = jnp.where(lane == 0, w1, jnp.where(lane == 1, w2, 0.0))
    counts = jnp.sum(both, axis=0, keepdims=True)
    cnt_ref[0] = jnp.broadcast_to(counts, cnt_ref.shape[1:]).astype(jnp.int32)


def _route(x2, g, sc, sh, w_r, b_r, seq):
    nt, d = x2.shape
    tm = ROW_TILE
    tps = seq // tm
    n_tiles = nt // tm
    row = lambda w: pl.BlockSpec((tm, w), lambda i: (i, 0))
    mod = pl.BlockSpec((1, 1, d), lambda i: (i // tps, 0, 0))
    lstrict = jnp.tril(jnp.ones((tm, tm), F32), k=-1)
    return pl.pallas_call(
        _route_kernel,
        grid=(n_tiles,),
        in_specs=[row(d), _full((1, d)), mod, mod, _full(w_r.shape), _full(b_r.shape), _full(lstrict.shape)],
        out_specs=[pl.BlockSpec((tm * TILE_SLABS, LANES), lambda i: (i, 0)), row(LANES), row(LANES),
                   pl.BlockSpec((1, SUBLANES, LANES), lambda i: (i, 0, 0))],
        out_shape=[jax.ShapeDtypeStruct((nt * TILE_SLABS, LANES), F32), jax.ShapeDtypeStruct((nt, LANES), jnp.int32),
                   jax.ShapeDtypeStruct((nt, LANES), F32),
                   jax.ShapeDtypeStruct((n_tiles, SUBLANES, LANES), jnp.int32)],
        compiler_params=_params(),
        name="route",
    )(x2, g, sc, sh, w_r, b_r, lstrict)


def _expert_kernel(bexp_ref, tok_ref, tok_next_ref, h_hbm, win_ref, wout_ref, o_ref, buf, sem, win_bf, wout_bf):
    i = pl.program_id(0)
    slot = _pipelined_row_gather(tok_ref, tok_next_ref, MOE_BLOCK, h_hbm, buf, sem)

    @pl.when(jnp.logical_or(i == 0, bexp_ref[i] != bexp_ref[jnp.maximum(i - 1, 0)]))
    def _():
        win_bf[...] = win_ref[0, 0].astype(BF16)
        wout_bf[...] = wout_ref[0, 0].astype(BF16)

    rows = buf.at[slot]
    gu = None
    for s in range(0, TILE_SLABS, 2):
        xs = jnp.concatenate([_slab(rows, 0, MOE_BLOCK, s), _slab(rows, 0, MOE_BLOCK, s + 1)], axis=1)
        part = _mm(xs, win_bf[s * LANES:(s + 2) * LANES, :])
        gu = part if gu is None else gu + part
    gate = gu[:, :D_EXPERT]
    up = gu[:, D_EXPERT:]
    _store_token_tiles(o_ref, _mm(gate * jax.nn.sigmoid(gate) * up, wout_bf[...]))


def _experts(block_exp, slot_row, h2, w_in_e, w_out_e, layer):
    n_blocks = block_exp.shape[0]
    rows = MOE_BLOCK
    d = TILE_SLABS * LANES
    tok3 = slot_row.reshape(n_blocks, 1, rows)
    smem_blk = lambda f: pl.BlockSpec((1, 1, rows), f, memory_space=pltpu.SMEM)
    grid_spec = pltpu.PrefetchScalarGridSpec(
        num_scalar_prefetch=1,
        grid=(n_blocks,),
        in_specs=[smem_blk(lambda i, be: (i, 0, 0)),
                  smem_blk(lambda i, be: (jnp.minimum(i + 1, n_blocks - 1), 0, 0)),
                  pl.BlockSpec(memory_space=pl.ANY),
                  pl.BlockSpec((1, 1, d, 2 * D_EXPERT), lambda i, be: (layer, be[i], 0, 0)),
                  pl.BlockSpec((1, 1, D_EXPERT, d), lambda i, be: (layer, be[i], 0, 0))],
        out_specs=pl.BlockSpec((rows * TILE_SLABS, LANES), lambda i, be: (i, 0)),
        scratch_shapes=[pltpu.VMEM((2, rows * TILE_SLABS, LANES), F32), pltpu.SemaphoreType.DMA((2,)),
                        pltpu.VMEM((d, 2 * D_EXPERT), BF16), pltpu.VMEM((D_EXPERT, d), BF16)],
    )
    return pl.pallas_call(
        _expert_kernel,
        grid_spec=grid_spec,
        out_shape=jax.ShapeDtypeStruct((n_blocks * rows * TILE_SLABS, LANES), F32),
        compiler_params=_params(),
        name="experts",
    )(block_exp, tok3, tok3, h2, w_in_e, w_out_e)


def _combine_kernel(pos_ref, pos_next_ref, x_ref, gt_ref, wgt_ref, yb_hbm, o_ref, buf, sem):
    o_ref[...] = _moe_residual(pos_ref, pos_next_ref, x_ref, gt_ref, wgt_ref, yb_hbm, buf, sem)


def _combine(pos, x2, gt, wgt, yb, seq):
    nt, d = x2.shape
    tm = ROW_TILE
    n_tiles = nt // tm
    pos3, specs, scratch = _moe_residual_specs(pos, tm, n_tiles, d, seq // tm)
    return pl.pallas_call(
        _combine_kernel,
        grid=(n_tiles,),
        in_specs=specs,
        out_specs=pl.BlockSpec((tm, d), lambda i: (i, 0)),
        out_shape=jax.ShapeDtypeStruct((nt, d), F32),
        scratch_shapes=scratch,
        compiler_params=_params(),
        name="combine",
    )(pos3, pos3, x2, gt, wgt, yb)


def _dispatch_plan(ridx, tile_counts, n_tokens):
    n_tiles = tile_counts.shape[0]
    tm = n_tokens // n_tiles
    na = n_tokens * TOP_K
    counts = tile_counts[:, 0, N_GROUPS:N_GROUPS + N_EXPERTS]
    tile_off = jnp.cumsum(counts, axis=0) - counts
    total = jnp.sum(counts, axis=0)
    padded = (total + MOE_BLOCK - 1) // MOE_BLOCK * MOE_BLOCK
    pad_ends = jnp.cumsum(padded)
    base = (pad_ends - padded)[None, :] + tile_off
    expert = ridx[:, :TOP_K].reshape(n_tiles, tm, TOP_K)
    rank = ridx[:, TOP_K:2 * TOP_K].reshape(n_tiles, tm, TOP_K)
    onehot = expert[..., None] == jnp.arange(N_EXPERTS, dtype=jnp.int32)
    dest = jnp.sum(jnp.where(onehot, base[:, None, None, :], 0), axis=-1) + rank
    dest = dest.reshape(n_tokens, TOP_K)
    n_blocks = -(-na // MOE_BLOCK) + N_EXPERTS
    slot_row = jnp.zeros((n_blocks * MOE_BLOCK,), jnp.int32).at[dest.reshape(-1)].set(
        jnp.arange(na, dtype=jnp.int32) // TOP_K * TILE_SLABS)
    block_start = jnp.arange(n_blocks, dtype=jnp.int32) * MOE_BLOCK
    block_exp = jnp.minimum(jnp.sum((pad_ends[None, :] <= block_start[:, None]).astype(jnp.int32), axis=1),
                            N_EXPERTS - 1)
    return block_exp, slot_row, dest * TILE_SLABS


def _pad_cols(w, width):
    return jnp.pad(w, ((0, 0), (0, width - w.shape[1])))


def _pad_heads(w, head_dim):
    rows = w.shape[0]
    w = w.reshape(rows, -1, head_dim)
    return jnp.pad(w, ((0, 0), (0, 0), (0, LANES - head_dim))).reshape(rows, -1)


def kernel(x, c, w_ada, b_ada, norm1_g, norm2_g, w_in, mu_a, w0_a, w2_a, a0_a, a2_a, g2_a, kk_a, ka_a, rk_a,
           lnx_g_a, lnx_b_a, conv_w_b, conv_b_b, bi_b, bf_b, hn_g_b, qn_g_c, kn_g_c, sink_c, w_br_a, w_br_b,
           w_br_c, w_out, w_rg, b_rg, w_re, b_re, w_e_in, w_e_out):
    bsz, seq, d = x.shape
    nt = bsz * seq
    depth = w_in.shape[0]
    assert bsz * A_HEADS * VAL_LANE_REPS == LANES, "the RWKV recurrence packs (batch, head, 4 value indices) on lanes"
    assert seq % ROW_TILE == 0 and seq % B_CHUNK == 0 and seq % SCAN_STEPS == 0

    mod = _ada_mod(c, w_ada, b_ada)
    hsum = jnp.kron(jnp.eye(A_HEADS, dtype=F32), jnp.ones((A_HEAD_DIM, A_HEAD_DIM), F32))
    tri = jnp.tril(jnp.ones((B_CHUNK, B_CHUNK), F32))

    o_b = 1024
    o_if = o_b + 4 * B_WIDTH
    o_cq = o_if + 2 * B_HEADS
    o_ckv = o_cq + C_WIDTH
    o_g = o_ckv + 2 * C_KV_WIDTH

    x2 = x.reshape(nt, d)
    moe = None
    for l in range(depth):
        m = mod[l].reshape(bsz, 6, 1, d)
        sh1, sc1, gt1, sh2, sc2, gt2 = (m[:, i] for i in range(6))
        wl = w_in[l]
        weights = [wl[:, o_g:].astype(BF16), wl[:, :o_b].astype(BF16), wl[:, o_b:o_if].astype(BF16),
                   _pad_heads(wl[:, o_cq:o_ckv], C_HEAD_DIM).astype(BF16),
                   _pad_heads(wl[:, o_ckv:o_g], C_HEAD_DIM).astype(BF16),
                   _pad_cols(wl[:, o_if:o_cq], LANES).astype(BF16)]
        outs = _norm_proj(x2, norm1_g[l][None], sc1, sh1, weights, [BF16] + [F32] * 5, seq, moe)
        if moe is not None:
            x2 = outs.pop(0)
        p_g, p_a, p_b, p_cq, p_ckv, p_if = outs

        wwa = (jnp.zeros((A_DECAY_LORA + A_ICL_LORA, 2 * A_WIDTH), F32)
               .at[:A_DECAY_LORA, :A_WIDTH].set(w2_a[l]).at[A_DECAY_LORA:, A_WIDTH:].set(a2_a[l]))
        r, dec, k2, v, kk, bvec, g = _rwkv_prep(p_a, mu_a[l][None], wwa, w0_a[l][None], a0_a[l][None], g2_a[l],
                                                kk_a[l][None], ka_a[l][None], hsum, seq)
        y_scan = _rwkv_scan(*(_key_layout(t, bsz, seq) for t in (dec, kk, bvec, k2, r)),
                            _value_layout(v, bsz, seq))
        y_scan = _value_unlayout(y_scan, bsz, seq)

        g_t = p_if[:, :SUBLANES].reshape(bsz, seq, SUBLANES).transpose(0, 2, 1)
        bif = jnp.concatenate([bi_b[l], bf_b[l]])
        y_b = _mlstm(p_b, p_if, g_t, conv_w_b[l], conv_b_b[l][None], _pad_cols(bif[None], LANES), bif[:, None],
                     hn_g_b[l].reshape(1, B_WIDTH), tri, tri.T, seq)

        y_c = _swa(p_cq, p_ckv, _pad_cols(qn_g_c[l][None], LANES), _pad_cols(kn_g_c[l][None], LANES),
                   _pad_cols(sink_c[l][None], LANES), seq)

        x2 = _merge(x2, gt1, p_g, y_scan, r, k2, v, g, y_b, y_c, hsum, lnx_g_a[l][None], lnx_b_a[l][None],
                    rk_a[l].reshape(1, A_WIDTH), w_br_a[l].astype(BF16), w_br_b[l].astype(BF16),
                    w_br_c[l].astype(BF16), w_out[l].astype(BF16), seq)

        w_r = _pad_cols(jnp.concatenate([w_rg[l], w_re[l]], axis=1), LANES)
        b_r = _pad_cols(jnp.concatenate([b_rg[l], b_re[l]])[None], LANES)
        h2, ridx, rwgt, tile_counts = _route(x2, norm2_g[l][None], sc2, sh2, w_r, b_r, seq)
        block_exp, slot_row, pos = _dispatch_plan(ridx, tile_counts, nt)
        yb = _experts(block_exp, slot_row, h2, w_e_in, w_e_out, l)
        moe = (pos, gt2, rwgt, yb)
    pos, gt2, rwgt, yb = moe
    x2 = _combine(pos, x2, gt2, rwgt, yb, seq)
    return x2.reshape(bsz, seq, d)
```

```python
import functools

import numpy as np
import jax
import jax.numpy as jnp
from jax import lax
from jax.experimental import pallas as pl
from jax.experimental.pallas import tpu as pltpu

F32 = jnp.float32
BF16 = jnp.bfloat16

A_HEADS = 4
A_HEAD_DIM = 64
A_WIDTH = 256
A_DECAY_LORA = 64
A_ICL_LORA = 64
A_LN_EPS = 64e-5
B_HEADS = 4
B_HEAD_DIM = 64
B_WIDTH = 256
B_CHUNK = 128
C_Q_HEADS = 8
C_KV_HEADS = 2
C_HEAD_DIM = 64
C_WIDTH = 512
C_KV_WIDTH = 128
WINDOW = 128
C_BLOCK = 128
N_GROUPS = 4
EXPERTS_PER_GROUP = 8
N_EXPERTS = 32
TOP_K = 2
D_EXPERT = 512
MOE_BLOCK = 128
EPS = 1e-6

LANES = 128
SUBLANES = 8
VMEM_LIMIT = 56 * 1024 * 1024

ROW_TILE = 256
SCAN_STEPS = 32
ADA_COL_TILE = 1536
SWA_STACK = 4


def _mm(a, b):
    return jnp.dot(a.astype(BF16), b.astype(BF16), preferred_element_type=F32)


def _mm32(a, b):
    return jnp.dot(a, b, precision=lax.Precision.HIGHEST, preferred_element_type=F32)


def _bf16_terms(x, n):
    terms = []
    for t in range(n):
        p = x.astype(BF16)
        terms.append(p)
        if t + 1 < n:
            x = x - p.astype(F32)
    return terms


def _sum_dots(pairs):
    out = None
    for a, b in reversed(pairs):
        t = jnp.dot(a, b, preferred_element_type=F32)
        out = t if out is None else out + t
    return out


def _mm_f32_lhs(a, b01):
    b = b01.astype(BF16)
    return _sum_dots([(p, b) for p in _bf16_terms(a, 3)])


def _mm3(a, b):
    a_hi, a_lo = _bf16_terms(a, 2)
    b_hi, b_lo = _bf16_terms(b, 2)
    return _sum_dots([(a_hi, b_hi), (a_hi, b_lo), (a_lo, b_hi)])


def _mm_nt(a, b):
    return lax.dot_general(a.astype(BF16), b.astype(BF16), (((1,), (1,)), ((), ())),
                           preferred_element_type=F32)


def _mm_tn(a, b):
    return lax.dot_general(a.astype(BF16), b.astype(BF16), (((0,), (0,)), ((), ())),
                           preferred_element_type=F32)


def _log_sigmoid(x):
    return jnp.minimum(x, 0.0) - jnp.log(1.0 + jnp.exp(-jnp.abs(x)))


def _params(n_axes=1):
    return pltpu.CompilerParams(dimension_semantics=("arbitrary",) * n_axes,
                                vmem_limit_bytes=VMEM_LIMIT)


def _full(shape):
    nd = len(shape)
    return pl.BlockSpec(shape, lambda *_: (0,) * nd)


def _ada_kernel(c_ref, w_ref, b_ref, o_ref):
    c = c_ref[...]
    cond = c * jax.nn.sigmoid(c)
    o_ref[0] = _mm(cond, w_ref[0]) + b_ref[0]


def _ada_mod(c, w_ada, b_ada):
    depth, d, n = w_ada.shape
    bsz = c.shape[0]
    tn = ADA_COL_TILE
    return pl.pallas_call(
        _ada_kernel,
        grid=(depth, n // tn),
        in_specs=[pl.BlockSpec((bsz, d), lambda l, j: (0, 0)),
                  pl.BlockSpec((1, d, tn), lambda l, j: (l, 0, j)),
                  pl.BlockSpec((1, 1, tn), lambda l, j: (l, 0, j))],
        out_specs=pl.BlockSpec((1, bsz, tn), lambda l, j: (l, 0, j)),
        out_shape=jax.ShapeDtypeStruct((depth, bsz, n), F32),
        compiler_params=_params(2),
        name="ada_mod",
    )(c, w_ada, b_ada.reshape(depth, 1, n))


TILE_SLABS = SUBLANES


def _slab(ref, first_token, n_tokens, s):
    return ref[pl.ds(first_token * TILE_SLABS + s, n_tokens, stride=TILE_SLABS), :]


def _store_token_tiles(ref, x):
    n = x.shape[0]
    for s in range(TILE_SLABS):
        ref[pl.ds(s, n, stride=TILE_SLABS), :] = x[:, s * LANES:(s + 1) * LANES]


def _start_row_gather(idx_ref, n_rows, src_hbm, dst, sem):
    for r in range(n_rows):
        first = pl.multiple_of(idx_ref[0, 0, r], TILE_SLABS)
        pltpu.make_async_copy(src_hbm.at[pl.ds(first, TILE_SLABS), :], dst.at[pl.ds(r * TILE_SLABS, TILE_SLABS), :],
                              sem).start(priority=r % 2)


def _wait_row_gather(n_rows, src_hbm, dst, sem):
    pltpu.make_async_copy(src_hbm.at[pl.ds(0, n_rows * TILE_SLABS), :], dst, sem).wait()


def _pipelined_row_gather(idx_ref, idx_next_ref, n_rows, src_hbm, buf, sem):
    i = pl.program_id(0)
    slot = i % 2

    @pl.when(i == 0)
    def _():
        _start_row_gather(idx_ref, n_rows, src_hbm, buf.at[0], sem.at[0])

    _start_row_gather(idx_next_ref, n_rows, src_hbm, buf.at[1 - slot], sem.at[1 - slot])
    _wait_row_gather(n_rows, src_hbm, buf.at[slot], sem.at[slot])

    @pl.when(i == pl.num_programs(0) - 1)
    def _():
        _wait_row_gather(n_rows, src_hbm, buf.at[1 - slot], sem.at[1 - slot])

    return slot


def _moe_residual(pos_ref, pos_next_ref, x_ref, gt_ref, wgt_ref, yb_hbm, buf, sem):
    tm = x_ref.shape[0]
    slot = _pipelined_row_gather(pos_ref, pos_next_ref, TOP_K * tm, yb_hbm, buf, sem)
    wgt = wgt_ref[...]
    w1, w2 = wgt[:, 0:1], wgt[:, 1:2]
    rows = buf.at[slot]
    gt = gt_ref[0]
    slabs = []
    for s in range(TILE_SLABS):
        lanes = slice(s * LANES, (s + 1) * LANES)
        y = _slab(rows, 0, tm, s) * w1 + _slab(rows, tm, tm, s) * w2
        slabs.append(x_ref[:, lanes] + gt[:, lanes] * y)
    return jnp.concatenate(slabs, axis=1)


def _moe_residual_specs(pos, tm, n_tiles, d, tps):
    pos3 = pos.reshape(n_tiles, tm, TOP_K).transpose(0, 2, 1).reshape(n_tiles, 1, TOP_K * tm)
    smem_blk = lambda f: pl.BlockSpec((1, 1, TOP_K * tm), f, memory_space=pltpu.SMEM)
    row = lambda w: pl.BlockSpec((tm, w), lambda i: (i, 0))
    specs = [smem_blk(lambda i: (i, 0, 0)),
             smem_blk(lambda i: (jnp.minimum(i + 1, n_tiles - 1), 0, 0)),
             row(d), pl.BlockSpec((1, 1, d), lambda i: (i // tps, 0, 0)), row(LANES),
             pl.BlockSpec(memory_space=pl.ANY)]
    assert d == TILE_SLABS * LANES
    scratch = [pltpu.VMEM((2, TOP_K * tm * TILE_SLABS, LANES), F32), pltpu.SemaphoreType.DMA((2,))]
    return pos3, specs, scratch


def _norm_proj_kernel(*refs, n_proj, moe):
    if moe:
        x = _moe_residual(*refs[:6], *refs[-2:])
        refs = refs[6:-2]
        refs[-1][...] = x
        refs = refs[:-1]
    else:
        x = refs[0][...]
        refs = refs[1:]
    g_ref, sc_ref, sh_ref = refs[:3]
    y = x * lax.rsqrt(jnp.mean(x * x, axis=-1, keepdims=True) + EPS) * g_ref[...]
    h = (y * (1.0 + sc_ref[0]) + sh_ref[0]).astype(BF16)
    for w_ref, o_ref in zip(refs[3:3 + n_proj], refs[3 + n_proj:]):
        o_ref[...] = jnp.dot(h, w_ref[...], preferred_element_type=F32).astype(o_ref.dtype)


def _norm_proj(x2, g, sc, sh, weights, out_dtypes, seq, moe=None):
    nt, d = x2.shape
    tm = ROW_TILE
    tps = seq // tm
    n_tiles = nt // tm
    row = lambda w: pl.BlockSpec((tm, w), lambda i: (i, 0))
    mod = pl.BlockSpec((1, 1, d), lambda i: (i // tps, 0, 0))
    in_specs = [_full((1, d)), mod, mod] + [_full(w.shape) for w in weights]
    out_specs = [row(w.shape[1]) for w in weights]
    out_shape = [jax.ShapeDtypeStruct((nt, w.shape[1]), dt) for w, dt in zip(weights, out_dtypes)]
    if moe is None:
        args, head_specs, scratch = [x2], [row(d)], []
    else:
        pos, gt, wgt, yb = moe
        pos3, head_specs, scratch = _moe_residual_specs(pos, tm, n_tiles, d, tps)
        args = [pos3, pos3, x2, gt, wgt, yb]
        out_specs.append(row(d))
        out_shape.append(jax.ShapeDtypeStruct((nt, d), F32))
    outs = pl.pallas_call(
        functools.partial(_norm_proj_kernel, n_proj=len(weights), moe=moe is not None),
        grid=(n_tiles,),
        in_specs=head_specs + in_specs,
        out_specs=out_specs,
        out_shape=out_shape,
        scratch_shapes=scratch,
        compiler_params=_params(),
        name="norm_proj_moe" if moe is not None else "norm_proj",
    )(*args, g, sc, sh, *weights)
    outs = list(outs)
    return outs if moe is None else [outs[-1]] + outs[:-1]


def _rwkv_prep_kernel(p_ref, prev_ref, mu_ref, wwa_ref, w0_ref, a0_ref, g2_ref, kkw_ref, kaw_ref, hsum_ref,
                      r_out, w_out, k_out, v_out, kk_out, b_out, g_out, *, tiles_per_seq):
    i = pl.program_id(0)
    p = p_ref[...]
    first = (i % tiles_per_seq) == 0
    prev_row = jnp.where(first, 0.0, prev_ref[SUBLANES - 1:SUBLANES, :])
    row = lax.broadcasted_iota(jnp.int32, p.shape, 0)
    shifted = jnp.where(row == 0, prev_row, pltpu.roll(p, 1, 0))
    pa = p + mu_ref[...] * (shifted - p)
    aw = A_WIDTH
    o_lora = 3 * aw
    o_glo = o_lora + A_DECAY_LORA + A_ICL_LORA
    r = pa[:, 0:aw]
    k = pa[:, aw:2 * aw]
    v = pa[:, 2 * aw:o_lora]
    slab = pa[:, o_lora:o_glo]
    glo = pa[:, o_glo:]
    lane = lax.broadcasted_iota(jnp.int32, slab.shape, 1)
    z = jnp.where(lane < A_DECAY_LORA, jnp.tanh(slab), slab)
    wa = _mm3(z, wwa_ref[...])
    w_log = _log_sigmoid(w0_ref[...] + wa[:, :aw]) - 0.5
    decay = jnp.exp(-jnp.exp(w_log))
    a = jax.nn.sigmoid(a0_ref[...] + wa[:, aw:])
    g = _mm3(jax.nn.sigmoid(glo), g2_ref[...])
    kk = k * kkw_ref[...]
    ss = _mm_f32_lhs(kk * kk, hsum_ref[...])
    kk = kk / jnp.maximum(jnp.sqrt(ss), 1e-12)
    r_out[...] = r
    w_out[...] = decay
    k_out[...] = k * (1.0 + (a - 1.0) * kaw_ref[...])
    v_out[...] = v
    kk_out[...] = kk
    b_out[...] = kk * a
    g_out[...] = g


def _time_major_spec(tm, width, tiles_per_seq):
    return pl.BlockSpec((tm, width), lambda i: (i % tiles_per_seq, i // tiles_per_seq))


def _rwkv_prep(p_a, mu, wwa, w0, a0, g2, kkw, kaw, hsum, seq):
    nt, wd = p_a.shape
    tm = min(ROW_TILE, seq)
    tps = seq // tm
    row = lambda w: pl.BlockSpec((tm, w), lambda i: (i, 0))
    prev = pl.BlockSpec((SUBLANES, wd), lambda i: (jnp.maximum(i * (tm // SUBLANES) - 1, 0), 0))
    bsz = nt // seq
    outs = [jax.ShapeDtypeStruct((seq, bsz * A_WIDTH), F32)] * 7
    return pl.pallas_call(
        functools.partial(_rwkv_prep_kernel, tiles_per_seq=tps),
        grid=(nt // tm,),
        in_specs=[row(wd), prev, _full(mu.shape), _full(wwa.shape), _full(w0.shape), _full(a0.shape),
                  _full(g2.shape), _full(kkw.shape), _full(kaw.shape), _full(hsum.shape)],
        out_specs=[_time_major_spec(tm, A_WIDTH, tps)] * 7,
        out_shape=outs,
        compiler_params=_params(),
        name="rwkv_prep",
    )(p_a, p_a, mu, wwa, w0, a0, g2, kkw, kaw, hsum)


VAL_LANE_REPS = 4
VAL_TILES = A_HEAD_DIM // (SUBLANES * VAL_LANE_REPS)
SUM_CHAINS = 4
STEP_PACK = 4


def _rwkv_scan_kernel(w_ref, kk_ref, b_ref, k_ref, r_ref, v_ref, expand_ref, y_ref, s_ref, e_ref, *, steps):
    @pl.when(pl.program_id(0) == 0)
    def _():
        s_ref[...] = jnp.zeros_like(s_ref)

    expand = expand_ref[...]
    groups = steps // STEP_PACK
    for op, ref in enumerate((w_ref, kk_ref, b_ref, k_ref, r_ref)):
        packed = ref[...].reshape(groups * A_HEAD_DIM, LANES)
        n_terms = 3 if ref is w_ref else 2
        rows = _sum_dots([(term, expand) for term in _bf16_terms(packed, n_terms)])
        e_ref[op] = rows.reshape(groups, A_HEAD_DIM, STEP_PACK * LANES)
    W, KK, BV, K, R = range(5)

    def tree(terms):
        while len(terms) > 1:
            terms = [terms[i] + terms[i + 1] for i in range(0, len(terms), 2)]
        return terms[0]

    def step(t, tq, lanes):
        def row(op, j):
            return jnp.broadcast_to(e_ref[op, tq, j:j + 1, lanes], (SUBLANES, LANES))

        tiles = range(VAL_TILES)
        v_t = [v_ref[t, ih * SUBLANES:(ih + 1) * SUBLANES, :] for ih in tiles]
        acc = [[None] * SUM_CHAINS for _ in tiles]
        for j in range(A_HEAD_DIM):
            kk = row(KK, j)
            for ih in tiles:
                term = s_ref[ih * A_HEAD_DIM + j] * kk
                c = j % SUM_CHAINS
                acc[ih][c] = term if acc[ih][c] is None else acc[ih][c] + term
        sa = [tree(a) for a in acc]
        y = [[None] * SUM_CHAINS for _ in tiles]
        for j in range(A_HEAD_DIM):
            w, b, k, r = (row(op, j) for op in (W, BV, K, R))
            for ih in tiles:
                idx = ih * A_HEAD_DIM + j
                s = s_ref[idx] * w - sa[ih] * b + v_t[ih] * k
                s_ref[idx] = s
                term = s * r
                c = j % SUM_CHAINS
                y[ih][c] = term if y[ih][c] is None else y[ih][c] + term
        for ih in tiles:
            y_ref[t, ih * SUBLANES:(ih + 1) * SUBLANES, :] = tree(y[ih])

    def packed_steps(tq, carry):
        for t4 in range(STEP_PACK):
            step(tq * STEP_PACK + t4, tq, slice(t4 * LANES, (t4 + 1) * LANES))
        return carry

    lax.fori_loop(0, steps // STEP_PACK, packed_steps, 0)


def _rwkv_scan(w_e, kk_e, b_e, k_e, r_e, v_p):
    seq = v_p.shape[0]
    tc = SCAN_STEPS
    src = jnp.arange(LANES)
    col = jnp.arange(STEP_PACK * LANES)
    group = LANES // STEP_PACK
    expand = ((src[:, None] // group == col[None, :] // LANES)
              & (src[:, None] % group == col[None, :] % LANES // VAL_LANE_REPS)).astype(BF16)
    key = pl.BlockSpec((tc // STEP_PACK, A_HEAD_DIM, LANES), lambda i: (i, 0, 0))
    val = pl.BlockSpec((tc, VAL_TILES * SUBLANES, LANES), lambda i: (i, 0, 0))
    return pl.pallas_call(
        functools.partial(_rwkv_scan_kernel, steps=tc),
        grid=(seq // tc,),
        in_specs=[key] * 5 + [val, _full(expand.shape)],
        out_specs=val,
        out_shape=jax.ShapeDtypeStruct((seq, VAL_TILES * SUBLANES, LANES), F32),
        scratch_shapes=[pltpu.VMEM((VAL_TILES * A_HEAD_DIM, SUBLANES, LANES), F32),
                        pltpu.VMEM((5, tc // STEP_PACK, A_HEAD_DIM, STEP_PACK * LANES), F32)],
        compiler_params=_params(),
        name="rwkv_scan",
    )(w_e, kk_e, b_e, k_e, r_e, v_p, expand)


def _key_layout(x, bsz, seq):
    x = x.reshape(seq // STEP_PACK, STEP_PACK * bsz * A_HEADS, A_HEAD_DIM)
    return x.transpose(0, 2, 1)


def _value_layout(x, bsz, seq):
    x = x.reshape(seq, bsz, A_HEADS, VAL_TILES, SUBLANES, VAL_LANE_REPS).transpose(0, 3, 4, 1, 2, 5)
    return x.reshape(seq, VAL_TILES * SUBLANES, LANES)


def _value_unlayout(y, bsz, seq):
    y = y.reshape(seq, VAL_TILES, SUBLANES, bsz, A_HEADS, VAL_LANE_REPS).transpose(0, 3, 4, 1, 2, 5)
    return y.reshape(seq, bsz * A_WIDTH)


def _mlstm_kernel(pb_ref, prev_ref, pif_ref, gt_ref, cw_ref, cb_ref, bifr_ref, bifc_ref, hng_ref,
                  ltri_ref, utri_ref, o_ref, c_ref, n_ref, m_ref, *, chunks_per_seq):
    i = pl.program_id(0)
    first = (i % chunks_per_seq) == 0

    @pl.when(first)
    def _():
        c_ref[...] = jnp.zeros_like(c_ref)
        n_ref[...] = jnp.zeros_like(n_ref)
        m_ref[...] = jnp.zeros_like(m_ref)

    L = B_CHUNK
    pb = pb_ref[...]
    x = pb[:, :2 * B_WIDTH]
    prev = jnp.where(first, 0.0, prev_ref[:, :2 * B_WIDTH])
    cw = cw_ref[...]
    taps = cw.shape[0]
    acc = x * cw[taps - 1:taps, :] + cb_ref[...]
    r8 = lax.broadcasted_iota(jnp.int32, prev.shape, 0)
    for s in range(1, taps):
        xs = pltpu.roll(x, s, 0)
        head = jnp.where(r8 < s, pltpu.roll(prev, s, 0), xs[:SUBLANES])
        xs = jnp.concatenate([head, xs[SUBLANES:]], axis=0)
        acc = acc + xs * cw[taps - 1 - s:taps - s, :]
    qk = acc * jax.nn.sigmoid(acc)
    q = qk[:, :B_WIDTH]
    k = qk[:, B_WIDTH:] * (B_HEAD_DIM ** -0.5)
    v = pb[:, 2 * B_WIDTH:3 * B_WIDTH]
    o = pb[:, 3 * B_WIDTH:]

    pif = pif_ref[...] + bifr_ref[...]
    lane = lax.broadcasted_iota(jnp.int32, pif.shape, 1)
    logf_c = jnp.where(lane >= B_HEADS, _log_sigmoid(pif), 0.0)
    f_c = _mm32(ltri_ref[...], logf_c)
    g = gt_ref[0] + bifc_ref[...]
    row = lax.broadcasted_iota(jnp.int32, g.shape, 0)
    logf_r = jnp.where(row >= B_HEADS, _log_sigmoid(g), 0.0)
    f_r = _mm32(logf_r, utri_ref[...])

    ti = lax.broadcasted_iota(jnp.int32, (L, L), 0)
    si = lax.broadcasted_iota(jnp.int32, (L, L), 1)
    causal = si <= ti

    W = B_WIDTH
    lane_head = lax.broadcasted_iota(jnp.int32, (L, W), 1) // B_HEAD_DIM
    lane_head_row = lax.broadcasted_iota(jnp.int32, (1, W), 1) // B_HEAD_DIM
    row_head_col = lax.broadcasted_iota(jnp.int32, (W, 1), 0) // B_HEAD_DIM
    same_head = (lax.broadcasted_iota(jnp.int32, (W, W), 0) // B_HEAD_DIM
                 == lax.broadcasted_iota(jnp.int32, (W, W), 1) // B_HEAD_DIM)
    c0 = c_ref[...]
    n0 = n_ref[...]
    q_c = _mm_nt(q, c0)
    q_n = q * n0
    inv_d = 1.0 / B_HEAD_DIM

    hh = jnp.zeros((L, W), F32)
    w_all = jnp.zeros((L, W), F32)
    cd_lane = jnp.zeros((1, W), F32)
    cd_rows = jnp.zeros((W, 1), F32)
    for h in range(B_HEADS):
        hm = lane_head == h
        fc = f_c[:, B_HEADS + h:B_HEADS + h + 1]
        igc = pif[:, h:h + 1]
        fr = f_r[B_HEADS + h:B_HEADS + h + 1, :]
        igr = g[h:h + 1, :]
        f_last = fr[:, L - 1:L]
        m0 = m_ref[h]

        d = jnp.where(causal, fc - fr + igr, -jnp.inf)
        g_inter = fc + m0
        m_t = jnp.maximum(g_inter, jnp.max(d, axis=-1, keepdims=True))
        sw = _mm_nt(jnp.where(hm, q, 0.0), k) * jnp.exp(d - m_t)
        inter = jnp.exp(g_inter - m_t)
        den = (jnp.sum(sw, axis=-1, keepdims=True)
               + inter * jnp.sum(jnp.where(hm, q_n, 0.0), axis=-1, keepdims=True))
        inv = 1.0 / jnp.maximum(jnp.abs(den), jnp.exp(-m_t))
        hh = jnp.where(hm, (_mm(sw, v) + inter * q_c) * inv, hh)

        g_end_r = f_last - fr + igr
        m_new = jnp.maximum(f_last + m0, jnp.max(g_end_r, axis=-1, keepdims=True))
        carry_decay = jnp.exp(f_last + m0 - m_new)
        w_all = jnp.where(hm, jnp.exp(f_last - fc + igc - m_new), w_all)
        cd_lane = jnp.where(lane_head_row == h, carry_decay, cd_lane)
        cd_rows = jnp.where(row_head_col == h, carry_decay, cd_rows)
        m_ref[h] = m_new

    sq = hh * hh
    ms = jnp.zeros((L, W), F32)
    for h in range(B_HEADS):
        hm = lane_head == h
        ms = jnp.where(hm, jnp.sum(jnp.where(hm, sq, 0.0), axis=-1, keepdims=True) * inv_d, ms)
    o_ref[...] = hh * lax.rsqrt(ms + EPS) * hng_ref[...] * jax.nn.sigmoid(o)

    c_ref[...] = cd_rows * c0 + jnp.where(same_head, _mm_tn(v * w_all, k), 0.0)
    n_ref[...] = cd_lane * n0 + jnp.sum(k * w_all, axis=0, keepdims=True)


def _mlstm(p_b, p_if, g_t, conv_w, conv_b, bif_row, bif_col, hn_g, ltri, utri, seq):
    nt = p_b.shape[0]
    L = B_CHUNK
    nc = seq // L
    return pl.pallas_call(
        functools.partial(_mlstm_kernel, chunks_per_seq=nc),
        grid=(nt // L,),
        in_specs=[pl.BlockSpec((L, 4 * B_WIDTH), lambda i: (i, 0)),
                  pl.BlockSpec((SUBLANES, 4 * B_WIDTH), lambda i: (jnp.maximum(i * (L // SUBLANES) - 1, 0), 0)),
                  pl.BlockSpec((L, LANES), lambda i: (i, 0)),
                  pl.BlockSpec((1, SUBLANES, L), lambda i: (i // nc, 0, i % nc)),
                  _full(conv_w.shape), _full(conv_b.shape), _full(bif_row.shape), _full(bif_col.shape),
                  _full(hn_g.shape), _full(ltri.shape), _full(utri.shape)],
        out_specs=pl.BlockSpec((L, B_WIDTH), lambda i: (i, 0)),
        out_shape=jax.ShapeDtypeStruct((nt, B_WIDTH), F32),
        scratch_shapes=[pltpu.VMEM((B_WIDTH, B_WIDTH), F32),
                        pltpu.VMEM((1, B_WIDTH), F32),
                        pltpu.VMEM((B_HEADS, 1, 1), F32)],
        compiler_params=_params(),
        name="mlstm",
    )(p_b, p_b, p_if, g_t, conv_w, conv_b, bif_row, bif_col, hn_g, ltri, utri)


def _swa_kernel(q_ref, kvc_ref, kvp_ref, qn_ref, kn_ref, sink_ref, o_ref, *, blocks_per_seq):
    i = pl.program_id(0)
    first = (i % blocks_per_seq) == 0
    bk = C_BLOCK
    hp = LANES
    group = SWA_STACK
    rows = group * bk
    kv = jnp.concatenate([kvp_ref[...], kvc_ref[...]], axis=0)
    qrow = lax.broadcasted_iota(jnp.int32, (rows, 2 * bk), 0)
    kj = lax.broadcasted_iota(jnp.int32, (rows, 2 * bk), 1)
    dist = qrow % bk + bk - kj
    valid = (dist >= 0) & (dist < WINDOW) & (kj >= jnp.where(first, bk, 0))
    distf = dist.astype(F32)
    rgroup = lax.broadcasted_iota(jnp.int32, (rows, 1), 0) // bk
    inv_d = 1.0 / C_HEAD_DIM
    qn = qn_ref[...]
    kn = kn_ref[...]

    def per_row_group(vals):
        out = vals[-1]
        for j in range(group - 2, -1, -1):
            out = jnp.where(rgroup == j, vals[j], out)
        return out

    ones_head = jnp.ones((hp, hp), F32)
    ones_keys = jnp.ones((2 * bk, hp), BF16)

    heads_per_kv = C_Q_HEADS // C_KV_HEADS
    kn_heads, v_heads = [], []
    for kvh in range(C_KV_HEADS):
        kh = kv[:, kvh * hp:(kvh + 1) * hp]
        kn_heads.append((kh * lax.rsqrt(_mm_f32_lhs(kh * kh, ones_head) * inv_d + EPS) * kn).astype(BF16))
        v_heads.append(kv[:, (C_KV_HEADS + kvh) * hp:(C_KV_HEADS + kvh + 1) * hp].astype(BF16))

    for first_head in range(0, C_Q_HEADS, group):
        heads = list(range(first_head, first_head + group))
        kh = kn_heads[first_head // heads_per_kv]
        vh = v_heads[first_head // heads_per_kv]
        q4 = jnp.concatenate([q_ref[:, h * hp:(h + 1) * hp] for h in heads], axis=0)
        q4 = q4 * lax.rsqrt(_mm_f32_lhs(q4 * q4, ones_head) * inv_d + EPS) * qn * (C_HEAD_DIM ** -0.5)
        slope = per_row_group([float(2.0 ** (-8.0 * (h + 1) / C_Q_HEADS)) for h in heads])
        sink = per_row_group([sink_ref[:, h:h + 1] for h in heads])
        s = jnp.where(valid, _mm_nt(q4, kh) - slope * distf, -jnp.inf)
        m = jnp.maximum(jnp.max(s, axis=-1, keepdims=True), sink)
        p = jnp.exp(s - m).astype(BF16)
        den = jnp.dot(p, ones_keys, preferred_element_type=F32) + jnp.exp(sink - m)
        out = jnp.dot(p, vh, preferred_element_type=F32) * (1.0 / den)
        for pair in range(group // 2):
            even = out[(2 * pair) * bk:(2 * pair + 1) * bk]
            odd = out[(2 * pair + 1) * bk:(2 * pair + 2) * bk]
            t = first_head // 2 + pair
            o_ref[:, t * hp:(t + 1) * hp] = even + pltpu.roll(odd, C_HEAD_DIM, 1)


def _swa(p_cq, p_ckv, qn_g, kn_g, sink_row, seq):
    nt = p_cq.shape[0]
    bk = C_BLOCK
    nb = seq // bk
    return pl.pallas_call(
        functools.partial(_swa_kernel, blocks_per_seq=nb),
        grid=(nt // bk,),
        in_specs=[pl.BlockSpec((bk, p_cq.shape[1]), lambda i: (i, 0)),
                  pl.BlockSpec((bk, p_ckv.shape[1]), lambda i: (i, 0)),
                  pl.BlockSpec((bk, p_ckv.shape[1]), lambda i: (jnp.maximum(i - 1, 0), 0)),
                  _full(qn_g.shape), _full(kn_g.shape), _full(sink_row.shape)],
        out_specs=pl.BlockSpec((bk, C_WIDTH), lambda i: (i, 0)),
        out_shape=jax.ShapeDtypeStruct((nt, C_WIDTH), F32),
        compiler_params=_params(),
        name="swa",
    )(p_cq, p_ckv, p_ckv, qn_g, kn_g, sink_row)


def _merge_kernel(x_ref, gt_ref, pg_ref, ya_ref, r_ref, k_ref, v_ref, g_ref, yb_ref, yc_ref,
                  hsum_ref, lng_ref, lnb_ref, rk_ref, wa_ref, wb_ref, wc_ref, wo_ref, o_ref):
    hsum = hsum_ref[...]
    inv_n = 1.0 / A_HEAD_DIM
    y = ya_ref[...]
    mu = _mm_f32_lhs(y, hsum) * inv_n
    dlt = y - mu
    var = _mm_f32_lhs(dlt * dlt, hsum) * inv_n
    yn = dlt * lax.rsqrt(var + A_LN_EPS) * lng_ref[...] + lnb_ref[...]
    bonus = _mm_f32_lhs(r_ref[...] * k_ref[...] * rk_ref[...], hsum) * v_ref[...]
    y_a = (yn + bonus) * g_ref[...]
    d = x_ref.shape[1]
    pg = pg_ref[...].astype(F32)
    merged = (jax.nn.sigmoid(pg[:, :d]) * _mm(y_a, wa_ref[...])
              + jax.nn.sigmoid(pg[:, d:2 * d]) * _mm(yb_ref[...], wb_ref[...])
              + jax.nn.sigmoid(pg[:, 2 * d:]) * _mm(yc_ref[...], wc_ref[...]))
    o_ref[...] = x_ref[...] + gt_ref[0] * _mm(merged, wo_ref[...])


def _merge(x2, gt, p_g, y_scan, r, k, v, g, y_b, y_c, hsum, lng, lnb, rk, wa, wb, wc, wo, seq):
    nt, d = x2.shape
    tm = ROW_TILE
    tps = seq // tm
    row = lambda w: pl.BlockSpec((tm, w), lambda i: (i, 0))
    consts = (hsum, lng, lnb, rk, wa, wb, wc, wo)
    return pl.pallas_call(
        _merge_kernel,
        grid=(nt // tm,),
        in_specs=[row(d), pl.BlockSpec((1, 1, d), lambda i: (i // tps, 0, 0)), row(3 * d)]
                 + [_time_major_spec(tm, A_WIDTH, tps)] * 5 + [row(B_WIDTH), row(C_WIDTH)]
                 + [_full(c.shape) for c in consts],
        out_specs=row(d),
        out_shape=jax.ShapeDtypeStruct((nt, d), F32),
        compiler_params=_params(),
        name="merge",
    )(x2, gt, p_g, y_scan, r, k, v, g, y_b, y_c, *consts)


def _route_kernel(x_ref, g_ref, sc_ref, sh_ref, wr_ref, br_ref, lstrict_ref, h_ref, idx_ref, wgt_ref, cnt_ref):
    x = x_ref[...]
    y = x * lax.rsqrt(jnp.mean(x * x, axis=-1, keepdims=True) + EPS) * g_ref[...]
    h = y * (1.0 + sc_ref[0]) + sh_ref[0]
    _store_token_tiles(h_ref, h)
    lg = _mm3(h, wr_ref[...]) + br_ref[...]
    lane = lax.broadcasted_iota(jnp.int32, lg.shape, 1)
    lanef = lane.astype(F32)
    ninf = -jnp.inf
    big = float(LANES)

    def first_argmax(vals, vmax):
        return jnp.min(jnp.where(vals == vmax, lanef, big), axis=-1, keepdims=True)

    gl = jnp.where(lane < N_GROUPS, lg, ninf)
    gmax = jnp.max(gl, axis=-1, keepdims=True)
    g_top = 1.0 / jnp.sum(jnp.exp(gl - gmax), axis=-1, keepdims=True)
    g_idx = first_argmax(gl, gmax)
    lo = N_GROUPS + EXPERTS_PER_GROUP * g_idx
    el = jnp.where((lanef >= lo) & (lanef < lo + EXPERTS_PER_GROUP), lg, ninf)
    e1 = jnp.max(el, axis=-1, keepdims=True)
    z = jnp.sum(jnp.exp(el - e1), axis=-1, keepdims=True)
    i1 = first_argmax(el, e1)
    el2 = jnp.where(lanef == i1, ninf, el)
    e2 = jnp.max(el2, axis=-1, keepdims=True)
    i2 = first_argmax(el2, e2)
    p1 = 1.0 / z
    p2 = jnp.exp(e2 - e1) / z
    w1 = g_top * p1 / (p1 + p2)
    w2 = g_top * p2 / (p1 + p2)
    oh1 = jnp.where(lanef == i1, 1.0, 0.0)
    oh2 = jnp.where(lanef == i2, 1.0, 0.0)
    both = oh1 + oh2
    before = _mm(lstrict_ref[...], both)
    r1 = jnp.sum(before * oh1, axis=-1, keepdims=True)
    r2 = jnp.sum(before * oh2, axis=-1, keepdims=True)
    idx = jnp.where(lane == 0, i1 - N_GROUPS, jnp.where(lane == 1, i2 - N_GROUPS,
                    jnp.where(lane == 2, r1, jnp.where(lane == 3, r2, 0.0))))
    idx_ref[...] = idx.astype(jnp.int32)
    wgt_ref[...] = jnp.where(lane == 0, w1, jnp.where(lane == 1, w2, 0.0))
    counts = jnp.sum(both, axis=0, keepdims=True)
    cnt_ref[0] = jnp.broadcast_to(counts, cnt_ref.shape[1:]).astype(jnp.int32)


def _route(x2, g, sc, sh, w_r, b_r, seq):
    nt, d = x2.shape
    tm = ROW_TILE
    tps = seq // tm
    n_tiles = nt // tm
    row = lambda w: pl.BlockSpec((tm, w), lambda i: (i, 0))
    mod = pl.BlockSpec((1, 1, d), lambda i: (i // tps, 0, 0))
    lstrict = jnp.tril(jnp.ones((tm, tm), F32), k=-1)
    return pl.pallas_call(
        _route_kernel,
        grid=(n_tiles,),
        in_specs=[row(d), _full((1, d)), mod, mod, _full(w_r.shape), _full(b_r.shape), _full(lstrict.shape)],
        out_specs=[pl.BlockSpec((tm * TILE_SLABS, LANES), lambda i: (i, 0)), row(LANES), row(LANES),
                   pl.BlockSpec((1, SUBLANES, LANES), lambda i: (i, 0, 0))],
        out_shape=[jax.ShapeDtypeStruct((nt * TILE_SLABS, LANES), F32), jax.ShapeDtypeStruct((nt, LANES), jnp.int32),
                   jax.ShapeDtypeStruct((nt, LANES), F32),
                   jax.ShapeDtypeStruct((n_tiles, SUBLANES, LANES), jnp.int32)],
        compiler_params=_params(),
        name="route",
    )(x2, g, sc, sh, w_r, b_r, lstrict)


def _expert_kernel(bexp_ref, tok_ref, tok_next_ref, h_hbm, win_ref, wout_ref, o_ref, buf, sem, win_bf, wout_bf):
    i = pl.program_id(0)
    slot = _pipelined_row_gather(tok_ref, tok_next_ref, MOE_BLOCK, h_hbm, buf, sem)

    @pl.when(jnp.logical_or(i == 0, bexp_ref[i] != bexp_ref[jnp.maximum(i - 1, 0)]))
    def _():
        win_bf[...] = win_ref[0, 0].astype(BF16)
        wout_bf[...] = wout_ref[0, 0].astype(BF16)

    rows = buf.at[slot]
    gu = None
    for s in range(0, TILE_SLABS, 2):
        xs = jnp.concatenate([_slab(rows, 0, MOE_BLOCK, s), _slab(rows, 0, MOE_BLOCK, s + 1)], axis=1)
        part = _mm(xs, win_bf[s * LANES:(s + 2) * LANES, :])
        gu = part if gu is None else gu + part
    gate = gu[:, :D_EXPERT]
    up = gu[:, D_EXPERT:]
    _store_token_tiles(o_ref, _mm(gate * jax.nn.sigmoid(gate) * up, wout_bf[...]))


def _experts(block_exp, slot_row, h2, w_in_e, w_out_e, layer):
    n_blocks = block_exp.shape[0]
    rows = MOE_BLOCK
    d = TILE_SLABS * LANES
    tok3 = slot_row.reshape(n_blocks, 1, rows)
    smem_blk = lambda f: pl.BlockSpec((1, 1, rows), f, memory_space=pltpu.SMEM)
    grid_spec = pltpu.PrefetchScalarGridSpec(
        num_scalar_prefetch=1,
        grid=(n_blocks,),
        in_specs=[smem_blk(lambda i, be: (i, 0, 0)),
                  smem_blk(lambda i, be: (jnp.minimum(i + 1, n_blocks - 1), 0, 0)),
                  pl.BlockSpec(memory_space=pl.ANY),
                  pl.BlockSpec((1, 1, d, 2 * D_EXPERT), lambda i, be: (layer, be[i], 0, 0)),
                  pl.BlockSpec((1, 1, D_EXPERT, d), lambda i, be: (layer, be[i], 0, 0))],
        out_specs=pl.BlockSpec((rows * TILE_SLABS, LANES), lambda i, be: (i, 0)),
        scratch_shapes=[pltpu.VMEM((2, rows * TILE_SLABS, LANES), F32), pltpu.SemaphoreType.DMA((2,)),
                        pltpu.VMEM((d, 2 * D_EXPERT), BF16), pltpu.VMEM((D_EXPERT, d), BF16)],
    )
    return pl.pallas_call(
        _expert_kernel,
        grid_spec=grid_spec,
        out_shape=jax.ShapeDtypeStruct((n_blocks * rows * TILE_SLABS, LANES), F32),
        compiler_params=_params(),
        name="experts",
    )(block_exp, tok3, tok3, h2, w_in_e, w_out_e)


def _combine_kernel(pos_ref, pos_next_ref, x_ref, gt_ref, wgt_ref, yb_hbm, o_ref, buf, sem):
    o_ref[...] = _moe_residual(pos_ref, pos_next_ref, x_ref, gt_ref, wgt_ref, yb_hbm, buf, sem)


def _combine(pos, x2, gt, wgt, yb, seq):
    nt, d = x2.shape
    tm = ROW_TILE
    n_tiles = nt // tm
    pos3, specs, scratch = _moe_residual_specs(pos, tm, n_tiles, d, seq // tm)
    return pl.pallas_call(
        _combine_kernel,
        grid=(n_tiles,),
        in_specs=specs,
        out_specs=pl.BlockSpec((tm, d), lambda i: (i, 0)),
        out_shape=jax.ShapeDtypeStruct((nt, d), F32),
        scratch_shapes=scratch,
        compiler_params=_params(),
        name="combine",
    )(pos3, pos3, x2, gt, wgt, yb)


def _slot_rows_kernel(dest_ref, out_ref):
    def clear(s, carry):
        out_ref[s] = 0
        return carry

    def put(a, carry):
        token = lax.shift_right_logical(a, TOP_K.bit_length() - 1)
        out_ref[dest_ref[a]] = token * TILE_SLABS
        return carry

    lax.fori_loop(0, out_ref.shape[0], clear, 0, unroll=8)
    lax.fori_loop(0, dest_ref.shape[0], put, 0, unroll=8)


def _slot_rows(dest_flat, n_slots):
    assert TOP_K & (TOP_K - 1) == 0
    smem = pl.BlockSpec(memory_space=pltpu.SMEM)
    return pl.pallas_call(
        _slot_rows_kernel,
        in_specs=[smem],
        out_specs=smem,
        out_shape=jax.ShapeDtypeStruct((n_slots,), jnp.int32),
        name="slot_rows",
    )(dest_flat)


def _dispatch_plan(ridx, tile_counts, n_tokens):
    n_tiles = tile_counts.shape[0]
    tm = n_tokens // n_tiles
    na = n_tokens * TOP_K
    counts = tile_counts[:, 0, N_GROUPS:N_GROUPS + N_EXPERTS]
    tile_off = jnp.cumsum(counts, axis=0) - counts
    total = jnp.sum(counts, axis=0)
    padded = (total + MOE_BLOCK - 1) // MOE_BLOCK * MOE_BLOCK
    pad_ends = jnp.cumsum(padded)
    base = (pad_ends - padded)[None, :] + tile_off
    expert = ridx[:, :TOP_K].reshape(n_tiles, tm, TOP_K)
    rank = ridx[:, TOP_K:2 * TOP_K].reshape(n_tiles, tm, TOP_K)
    onehot = expert[..., None] == jnp.arange(N_EXPERTS, dtype=jnp.int32)
    dest = jnp.sum(jnp.where(onehot, base[:, None, None, :], 0), axis=-1) + rank
    dest = dest.reshape(n_tokens, TOP_K)
    n_blocks = -(-na // MOE_BLOCK) + N_EXPERTS
    slot_row = _slot_rows(dest.reshape(-1), n_blocks * MOE_BLOCK)
    block_start = jnp.arange(n_blocks, dtype=jnp.int32) * MOE_BLOCK
    block_exp = jnp.minimum(jnp.sum((pad_ends[None, :] <= block_start[:, None]).astype(jnp.int32), axis=1),
                            N_EXPERTS - 1)
    return block_exp, slot_row, dest * TILE_SLABS


def _pad_cols(w, width):
    return jnp.pad(w, ((0, 0), (0, width - w.shape[1])))


def _pad_heads(w, head_dim):
    rows = w.shape[0]
    w = w.reshape(rows, -1, head_dim)
    return jnp.pad(w, ((0, 0), (0, 0), (0, LANES - head_dim))).reshape(rows, -1)


def kernel(x, c, w_ada, b_ada, norm1_g, norm2_g, w_in, mu_a, w0_a, w2_a, a0_a, a2_a, g2_a, kk_a, ka_a, rk_a,
           lnx_g_a, lnx_b_a, conv_w_b, conv_b_b, bi_b, bf_b, hn_g_b, qn_g_c, kn_g_c, sink_c, w_br_a, w_br_b,
           w_br_c, w_out, w_rg, b_rg, w_re, b_re, w_e_in, w_e_out):
    bsz, seq, d = x.shape
    nt = bsz * seq
    depth = w_in.shape[0]
    assert bsz * A_HEADS * VAL_LANE_REPS == LANES, "the RWKV recurrence packs (batch, head, 4 value indices) on lanes"
    assert seq % ROW_TILE == 0 and seq % B_CHUNK == 0 and seq % SCAN_STEPS == 0

    mod = _ada_mod(c, w_ada, b_ada)
    hsum = jnp.kron(jnp.eye(A_HEADS, dtype=F32), jnp.ones((A_HEAD_DIM, A_HEAD_DIM), F32))
    tri = jnp.tril(jnp.ones((B_CHUNK, B_CHUNK), F32))

    o_b = 1024
    o_if = o_b + 4 * B_WIDTH
    o_cq = o_if + 2 * B_HEADS
    o_ckv = o_cq + C_WIDTH
    o_g = o_ckv + 2 * C_KV_WIDTH

    x2 = x.reshape(nt, d)
    moe = None
    for l in range(depth):
        m = mod[l].reshape(bsz, 6, 1, d)
        sh1, sc1, gt1, sh2, sc2, gt2 = (m[:, i] for i in range(6))
        wl = w_in[l]
        weights = [wl[:, o_g:].astype(BF16), wl[:, :o_b].astype(BF16), wl[:, o_b:o_if].astype(BF16),
                   _pad_heads(wl[:, o_cq:o_ckv], C_HEAD_DIM).astype(BF16),
                   _pad_heads(wl[:, o_ckv:o_g], C_HEAD_DIM).astype(BF16),
                   _pad_cols(wl[:, o_if:o_cq], LANES).astype(BF16)]
        outs = _norm_proj(x2, norm1_g[l][None], sc1, sh1, weights, [BF16] + [F32] * 5, seq, moe)
        if moe is not None:
            x2 = outs.pop(0)
        p_g, p_a, p_b, p_cq, p_ckv, p_if = outs

        wwa = (jnp.zeros((A_DECAY_LORA + A_ICL_LORA, 2 * A_WIDTH), F32)
               .at[:A_DECAY_LORA, :A_WIDTH].set(w2_a[l]).at[A_DECAY_LORA:, A_WIDTH:].set(a2_a[l]))
        r, dec, k2, v, kk, bvec, g = _rwkv_prep(p_a, mu_a[l][None], wwa, w0_a[l][None], a0_a[l][None], g2_a[l],
                                                kk_a[l][None], ka_a[l][None], hsum, seq)
        y_scan = _rwkv_scan(*(_key_layout(t, bsz, seq) for t in (dec, kk, bvec, k2, r)),
                            _value_layout(v, bsz, seq))
        y_scan = _value_unlayout(y_scan, bsz, seq)

        g_t = p_if[:, :SUBLANES].reshape(bsz, seq, SUBLANES).transpose(0, 2, 1)
        bif = jnp.concatenate([bi_b[l], bf_b[l]])
        y_b = _mlstm(p_b, p_if, g_t, conv_w_b[l], conv_b_b[l][None], _pad_cols(bif[None], LANES), bif[:, None],
                     hn_g_b[l].reshape(1, B_WIDTH), tri, tri.T, seq)

        y_c = _swa(p_cq, p_ckv, _pad_cols(qn_g_c[l][None], LANES), _pad_cols(kn_g_c[l][None], LANES),
                   _pad_cols(sink_c[l][None], LANES), seq)

        x2 = _merge(x2, gt1, p_g, y_scan, r, k2, v, g, y_b, y_c, hsum, lnx_g_a[l][None], lnx_b_a[l][None],
                    rk_a[l].reshape(1, A_WIDTH), w_br_a[l].astype(BF16), w_br_b[l].astype(BF16),
                    w_br_c[l].astype(BF16), w_out[l].astype(BF16), seq)

        w_r = _pad_cols(jnp.concatenate([w_rg[l], w_re[l]], axis=1), LANES)
        b_r = _pad_cols(jnp.concatenate([b_rg[l], b_re[l]])[None], LANES)
        h2, ridx, rwgt, tile_counts = _route(x2, norm2_g[l][None], sc2, sh2, w_r, b_r, seq)
        block_exp, slot_row, pos = _dispatch_plan(ridx, tile_counts, nt)
        yb = _experts(block_exp, slot_row, h2, w_e_in, w_e_out, l)
        moe = (pos, gt2, rwgt, yb)
    pos, gt2, rwgt, yb = moe
    x2 = _combine(pos, x2, gt2, rwgt, yb, seq)
    return x2.reshape(bsz, seq, d)
```

```python
import functools

import numpy as np
import jax
import jax.numpy as jnp
from jax import lax
from jax.experimental import pallas as pl
from jax.experimental.pallas import tpu as pltpu

F32 = jnp.float32
BF16 = jnp.bfloat16

A_HEADS = 4
A_HEAD_DIM = 64
A_WIDTH = 256
A_DECAY_LORA = 64
A_ICL_LORA = 64
A_LN_EPS = 64e-5
B_HEADS = 4
B_HEAD_DIM = 64
B_WIDTH = 256
B_CHUNK = 128
C_Q_HEADS = 8
C_KV_HEADS = 2
C_HEAD_DIM = 64
C_WIDTH = 512
C_KV_WIDTH = 128
WINDOW = 128
C_BLOCK = 128
N_GROUPS = 4
EXPERTS_PER_GROUP = 8
N_EXPERTS = 32
TOP_K = 2
D_EXPERT = 512
MOE_BLOCK = 128
EPS = 1e-6

LANES = 128
SUBLANES = 8
VMEM_LIMIT = 56 * 1024 * 1024

ROW_TILE = 256
SCAN_STEPS = 32
ADA_COL_TILE = 1536
SWA_STACK = 4


def _mm(a, b):
    return jnp.dot(a.astype(BF16), b.astype(BF16), preferred_element_type=F32)


def _mm32(a, b):
    return jnp.dot(a, b, precision=lax.Precision.HIGHEST, preferred_element_type=F32)


def _bf16_terms(x, n):
    terms = []
    for t in range(n):
        p = x.astype(BF16)
        terms.append(p)
        if t + 1 < n:
            x = x - p.astype(F32)
    return terms


def _sum_dots(pairs):
    out = None
    for a, b in reversed(pairs):
        t = jnp.dot(a, b, preferred_element_type=F32)
        out = t if out is None else out + t
    return out


def _mm_f32_lhs(a, b01):
    b = b01.astype(BF16)
    return _sum_dots([(p, b) for p in _bf16_terms(a, 3)])


def _mm3(a, b):
    a_hi, a_lo = _bf16_terms(a, 2)
    b_hi, b_lo = _bf16_terms(b, 2)
    return _sum_dots([(a_hi, b_hi), (a_hi, b_lo), (a_lo, b_hi)])


def _mm_nt(a, b):
    return lax.dot_general(a.astype(BF16), b.astype(BF16), (((1,), (1,)), ((), ())),
                           preferred_element_type=F32)


def _mm_tn(a, b):
    return lax.dot_general(a.astype(BF16), b.astype(BF16), (((0,), (0,)), ((), ())),
                           preferred_element_type=F32)


def _log_sigmoid(x):
    return jnp.minimum(x, 0.0) - jnp.log(1.0 + jnp.exp(-jnp.abs(x)))


def _params(n_axes=1):
    return pltpu.CompilerParams(dimension_semantics=("arbitrary",) * n_axes,
                                vmem_limit_bytes=VMEM_LIMIT)


def _full(shape):
    nd = len(shape)
    return pl.BlockSpec(shape, lambda *_: (0,) * nd)


def _ada_kernel(c_ref, w_ref, b_ref, o_ref):
    c = c_ref[...]
    cond = c * jax.nn.sigmoid(c)
    o_ref[0] = _mm(cond, w_ref[0]) + b_ref[0]


def _ada_mod(c, w_ada, b_ada):
    depth, d, n = w_ada.shape
    bsz = c.shape[0]
    tn = ADA_COL_TILE
    return pl.pallas_call(
        _ada_kernel,
        grid=(depth, n // tn),
        in_specs=[pl.BlockSpec((bsz, d), lambda l, j: (0, 0)),
                  pl.BlockSpec((1, d, tn), lambda l, j: (l, 0, j)),
                  pl.BlockSpec((1, 1, tn), lambda l, j: (l, 0, j))],
        out_specs=pl.BlockSpec((1, bsz, tn), lambda l, j: (l, 0, j)),
        out_shape=jax.ShapeDtypeStruct((depth, bsz, n), F32),
        compiler_params=_params(2),
        name="ada_mod",
    )(c, w_ada, b_ada.reshape(depth, 1, n))


TILE_SLABS = SUBLANES


def _slab(ref, first_token, n_tokens, s):
    return ref[pl.ds(first_token * TILE_SLABS + s, n_tokens, stride=TILE_SLABS), :]


def _store_token_tiles(ref, x):
    n = x.shape[0]
    for s in range(TILE_SLABS):
        ref[pl.ds(s, n, stride=TILE_SLABS), :] = x[:, s * LANES:(s + 1) * LANES]


def _start_row_gather(idx_ref, n_rows, src_hbm, dst, sem):
    for r in range(n_rows):
        first = pl.multiple_of(idx_ref[0, 0, r], TILE_SLABS)
        pltpu.make_async_copy(src_hbm.at[pl.ds(first, TILE_SLABS), :], dst.at[pl.ds(r * TILE_SLABS, TILE_SLABS), :],
                              sem).start(priority=r % 2)


def _wait_row_gather(n_rows, src_hbm, dst, sem):
    pltpu.make_async_copy(src_hbm.at[pl.ds(0, n_rows * TILE_SLABS), :], dst, sem).wait()


def _with_gathered_rows(idx_ref, idx_next_ref, n_rows, src_hbm, bufs, sem, body):
    i = pl.program_id(0)

    @pl.when(i == 0)
    def _():
        _start_row_gather(idx_ref, n_rows, src_hbm, bufs[0], sem.at[0])

    for parity in (0, 1):
        @pl.when(i % 2 == parity)
        def _():
            cur, nxt = bufs[parity], bufs[1 - parity]
            _wait_row_gather(n_rows, src_hbm, cur, sem.at[parity])
            _start_row_gather(idx_next_ref, n_rows, src_hbm, nxt, sem.at[1 - parity])
            body(cur)

            @pl.when(i == pl.num_programs(0) - 1)
            def _():
                _wait_row_gather(n_rows, src_hbm, nxt, sem.at[1 - parity])


def _moe_residual(pos_ref, pos_next_ref, x_ref, gt_ref, wgt_ref, yb_hbm, bufs, sem, emit):
    tm = x_ref.shape[0]

    def body(rows):
        wgt = wgt_ref[...]
        w1, w2 = wgt[:, 0:1], wgt[:, 1:2]
        gt = gt_ref[0]
        slabs = []
        for s in range(TILE_SLABS):
            lanes = slice(s * LANES, (s + 1) * LANES)
            y = _slab(rows, 0, tm, s) * w1 + _slab(rows, tm, tm, s) * w2
            slabs.append(x_ref[:, lanes] + gt[:, lanes] * y)
        emit(jnp.concatenate(slabs, axis=1))

    _with_gathered_rows(pos_ref, pos_next_ref, TOP_K * tm, yb_hbm, bufs, sem, body)


def _moe_residual_specs(pos, tm, n_tiles, d, tps):
    pos3 = pos.reshape(n_tiles, tm, TOP_K).transpose(0, 2, 1).reshape(n_tiles, 1, TOP_K * tm)
    smem_blk = lambda f: pl.BlockSpec((1, 1, TOP_K * tm), f, memory_space=pltpu.SMEM)
    row = lambda w: pl.BlockSpec((tm, w), lambda i: (i, 0))
    specs = [smem_blk(lambda i: (i, 0, 0)),
             smem_blk(lambda i: (jnp.minimum(i + 1, n_tiles - 1), 0, 0)),
             row(d), pl.BlockSpec((1, 1, d), lambda i: (i // tps, 0, 0)), row(LANES),
             pl.BlockSpec(memory_space=pl.ANY)]
    assert d == TILE_SLABS * LANES
    scratch = [pltpu.VMEM((TOP_K * tm * TILE_SLABS, LANES), F32)] * 2 + [pltpu.SemaphoreType.DMA((2,))]
    return pos3, specs, scratch


def _norm_proj_kernel(*refs, n_proj, moe):
    def project(x, refs):
        g_ref, sc_ref, sh_ref = refs[:3]
        y = x * lax.rsqrt(jnp.mean(x * x, axis=-1, keepdims=True) + EPS) * g_ref[...]
        h = (y * (1.0 + sc_ref[0]) + sh_ref[0]).astype(BF16)
        for w_ref, o_ref in zip(refs[3:3 + n_proj], refs[3 + n_proj:]):
            o_ref[...] = jnp.dot(h, w_ref[...], preferred_element_type=F32).astype(o_ref.dtype)

    if moe:
        rest = refs[6:-3]

        def emit(x):
            rest[-1][...] = x
            project(x, rest[:-1])

        _moe_residual(*refs[:6], refs[-3:-1], refs[-1], emit)
    else:
        project(refs[0][...], refs[1:])


def _norm_proj(x2, g, sc, sh, weights, out_dtypes, seq, moe=None):
    nt, d = x2.shape
    tm = ROW_TILE
    tps = seq // tm
    n_tiles = nt // tm
    row = lambda w: pl.BlockSpec((tm, w), lambda i: (i, 0))
    mod = pl.BlockSpec((1, 1, d), lambda i: (i // tps, 0, 0))
    in_specs = [_full((1, d)), mod, mod] + [_full(w.shape) for w in weights]
    out_specs = [row(w.shape[1]) for w in weights]
    out_shape = [jax.ShapeDtypeStruct((nt, w.shape[1]), dt) for w, dt in zip(weights, out_dtypes)]
    if moe is None:
        args, head_specs, scratch = [x2], [row(d)], []
    else:
        pos, gt, wgt, yb = moe
        pos3, head_specs, scratch = _moe_residual_specs(pos, tm, n_tiles, d, tps)
        args = [pos3, pos3, x2, gt, wgt, yb]
        out_specs.append(row(d))
        out_shape.append(jax.ShapeDtypeStruct((nt, d), F32))
    outs = pl.pallas_call(
        functools.partial(_norm_proj_kernel, n_proj=len(weights), moe=moe is not None),
        grid=(n_tiles,),
        in_specs=head_specs + in_specs,
        out_specs=out_specs,
        out_shape=out_shape,
        scratch_shapes=scratch,
        compiler_params=_params(),
        name="norm_proj_moe" if moe is not None else "norm_proj",
    )(*args, g, sc, sh, *weights)
    outs = list(outs)
    return outs if moe is None else [outs[-1]] + outs[:-1]


def _rwkv_prep_kernel(p_ref, prev_ref, mu_ref, wwa_ref, w0_ref, a0_ref, g2_ref, kkw_ref, kaw_ref, hsum_ref,
                      r_out, w_out, k_out, v_out, kk_out, b_out, g_out, *, tiles_per_seq):
    i = pl.program_id(0)
    p = p_ref[...]
    first = (i % tiles_per_seq) == 0
    prev_row = jnp.where(first, 0.0, prev_ref[SUBLANES - 1:SUBLANES, :])
    row = lax.broadcasted_iota(jnp.int32, p.shape, 0)
    shifted = jnp.where(row == 0, prev_row, pltpu.roll(p, 1, 0))
    pa = p + mu_ref[...] * (shifted - p)
    aw = A_WIDTH
    o_lora = 3 * aw
    o_glo = o_lora + A_DECAY_LORA + A_ICL_LORA
    r = pa[:, 0:aw]
    k = pa[:, aw:2 * aw]
    v = pa[:, 2 * aw:o_lora]
    slab = pa[:, o_lora:o_glo]
    glo = pa[:, o_glo:]
    lane = lax.broadcasted_iota(jnp.int32, slab.shape, 1)
    z = jnp.where(lane < A_DECAY_LORA, jnp.tanh(slab), slab)
    wa = _mm3(z, wwa_ref[...])
    w_log = _log_sigmoid(w0_ref[...] + wa[:, :aw]) - 0.5
    decay = jnp.exp(-jnp.exp(w_log))
    a = jax.nn.sigmoid(a0_ref[...] + wa[:, aw:])
    g = _mm3(jax.nn.sigmoid(glo), g2_ref[...])
    kk = k * kkw_ref[...]
    ss = _mm_f32_lhs(kk * kk, hsum_ref[...])
    kk = kk / jnp.maximum(jnp.sqrt(ss), 1e-12)
    r_out[...] = r
    w_out[...] = decay
    k_out[...] = k * (1.0 + (a - 1.0) * kaw_ref[...])
    v_out[...] = v
    kk_out[...] = kk
    b_out[...] = kk * a
    g_out[...] = g


def _time_major_spec(tm, width, tiles_per_seq):
    return pl.BlockSpec((tm, width), lambda i: (i % tiles_per_seq, i // tiles_per_seq))


def _rwkv_prep(p_a, mu, wwa, w0, a0, g2, kkw, kaw, hsum, seq):
    nt, wd = p_a.shape
    tm = min(ROW_TILE, seq)
    tps = seq // tm
    row = lambda w: pl.BlockSpec((tm, w), lambda i: (i, 0))
    prev = pl.BlockSpec((SUBLANES, wd), lambda i: (jnp.maximum(i * (tm // SUBLANES) - 1, 0), 0))
    bsz = nt // seq
    outs = [jax.ShapeDtypeStruct((seq, bsz * A_WIDTH), F32)] * 7
    return pl.pallas_call(
        functools.partial(_rwkv_prep_kernel, tiles_per_seq=tps),
        grid=(nt // tm,),
        in_specs=[row(wd), prev, _full(mu.shape), _full(wwa.shape), _full(w0.shape), _full(a0.shape),
                  _full(g2.shape), _full(kkw.shape), _full(kaw.shape), _full(hsum.shape)],
        out_specs=[_time_major_spec(tm, A_WIDTH, tps)] * 7,
        out_shape=outs,
        compiler_params=_params(),
        name="rwkv_prep",
    )(p_a, p_a, mu, wwa, w0, a0, g2, kkw, kaw, hsum)


VAL_LANE_REPS = 4
VAL_TILES = A_HEAD_DIM // (SUBLANES * VAL_LANE_REPS)
SUM_CHAINS = 4
STEP_PACK = 4


def _rwkv_scan_kernel(w_ref, kk_ref, b_ref, k_ref, r_ref, v_ref, expand_ref, y_ref, s_ref, e_ref, *, steps):
    @pl.when(pl.program_id(0) == 0)
    def _():
        s_ref[...] = jnp.zeros_like(s_ref)

    expand = expand_ref[...]
    groups = steps // STEP_PACK
    for op, ref in enumerate((w_ref, kk_ref, b_ref, k_ref, r_ref)):
        packed = ref[...].reshape(groups * A_HEAD_DIM, LANES)
        n_terms = 3 if ref is w_ref else 2
        rows = _sum_dots([(term, expand) for term in _bf16_terms(packed, n_terms)])
        e_ref[op] = rows.reshape(groups, A_HEAD_DIM, STEP_PACK * LANES)
    W, KK, BV, K, R = range(5)

    def tree(terms):
        while len(terms) > 1:
            terms = [terms[i] + terms[i + 1] for i in range(0, len(terms), 2)]
        return terms[0]

    def step(t, tq, lanes):
        def row(op, j):
            return jnp.broadcast_to(e_ref[op, tq, j:j + 1, lanes], (SUBLANES, LANES))

        tiles = range(VAL_TILES)
        v_t = [v_ref[t, ih * SUBLANES:(ih + 1) * SUBLANES, :] for ih in tiles]
        acc = [[None] * SUM_CHAINS for _ in tiles]
        for j in range(A_HEAD_DIM):
            kk = row(KK, j)
            for ih in tiles:
                term = s_ref[ih * A_HEAD_DIM + j] * kk
                c = j % SUM_CHAINS
                acc[ih][c] = term if acc[ih][c] is None else acc[ih][c] + term
        sa = [tree(a) for a in acc]
        y = [[None] * SUM_CHAINS for _ in tiles]
        for j in range(A_HEAD_DIM):
            w, b, k, r = (row(op, j) for op in (W, BV, K, R))
            for ih in tiles:
                idx = ih * A_HEAD_DIM + j
                s = s_ref[idx] * w - sa[ih] * b + v_t[ih] * k
                s_ref[idx] = s
                term = s * r
                c = j % SUM_CHAINS
                y[ih][c] = term if y[ih][c] is None else y[ih][c] + term
        for ih in tiles:
            y_ref[t, ih * SUBLANES:(ih + 1) * SUBLANES, :] = tree(y[ih])

    def packed_steps(tq, carry):
        for t4 in range(STEP_PACK):
            step(tq * STEP_PACK + t4, tq, slice(t4 * LANES, (t4 + 1) * LANES))
        return carry

    lax.fori_loop(0, steps // STEP_PACK, packed_steps, 0)


def _rwkv_scan(w_e, kk_e, b_e, k_e, r_e, v_p):
    seq = v_p.shape[0]
    tc = SCAN_STEPS
    src = jnp.arange(LANES)
    col = jnp.arange(STEP_PACK * LANES)
    group = LANES // STEP_PACK
    expand = ((src[:, None] // group == col[None, :] // LANES)
              & (src[:, None] % group == col[None, :] % LANES // VAL_LANE_REPS)).astype(BF16)
    key = pl.BlockSpec((tc // STEP_PACK, A_HEAD_DIM, LANES), lambda i: (i, 0, 0))
    val = pl.BlockSpec((tc, VAL_TILES * SUBLANES, LANES), lambda i: (i, 0, 0))
    return pl.pallas_call(
        functools.partial(_rwkv_scan_kernel, steps=tc),
        grid=(seq // tc,),
        in_specs=[key] * 5 + [val, _full(expand.shape)],
        out_specs=val,
        out_shape=jax.ShapeDtypeStruct((seq, VAL_TILES * SUBLANES, LANES), F32),
        scratch_shapes=[pltpu.VMEM((VAL_TILES * A_HEAD_DIM, SUBLANES, LANES), F32),
                        pltpu.VMEM((5, tc // STEP_PACK, A_HEAD_DIM, STEP_PACK * LANES), F32)],
        compiler_params=_params(),
        name="rwkv_scan",
    )(w_e, kk_e, b_e, k_e, r_e, v_p, expand)


def _key_layout(x, bsz, seq):
    x = x.reshape(seq // STEP_PACK, STEP_PACK * bsz * A_HEADS, A_HEAD_DIM)
    return x.transpose(0, 2, 1)


def _value_layout(x, bsz, seq):
    x = x.reshape(seq, bsz, A_HEADS, VAL_TILES, SUBLANES, VAL_LANE_REPS).transpose(0, 3, 4, 1, 2, 5)
    return x.reshape(seq, VAL_TILES * SUBLANES, LANES)


def _value_unlayout(y, bsz, seq):
    y = y.reshape(seq, VAL_TILES, SUBLANES, bsz, A_HEADS, VAL_LANE_REPS).transpose(0, 3, 4, 1, 2, 5)
    return y.reshape(seq, bsz * A_WIDTH)


def _mlstm_kernel(pb_ref, prev_ref, pif_ref, gt_ref, cw_ref, cb_ref, bifr_ref, bifc_ref, hng_ref,
                  ltri_ref, utri_ref, o_ref, c_ref, n_ref, m_ref, *, chunks_per_seq):
    i = pl.program_id(0)
    first = (i % chunks_per_seq) == 0

    @pl.when(first)
    def _():
        c_ref[...] = jnp.zeros_like(c_ref)
        n_ref[...] = jnp.zeros_like(n_ref)
        m_ref[...] = jnp.zeros_like(m_ref)

    L = B_CHUNK
    pb = pb_ref[...]
    x = pb[:, :2 * B_WIDTH]
    prev = jnp.where(first, 0.0, prev_ref[:, :2 * B_WIDTH])
    cw = cw_ref[...]
    taps = cw.shape[0]
    acc = x * cw[taps - 1:taps, :] + cb_ref[...]
    r8 = lax.broadcasted_iota(jnp.int32, prev.shape, 0)
    for s in range(1, taps):
        xs = pltpu.roll(x, s, 0)
        head = jnp.where(r8 < s, pltpu.roll(prev, s, 0), xs[:SUBLANES])
        xs = jnp.concatenate([head, xs[SUBLANES:]], axis=0)
        acc = acc + xs * cw[taps - 1 - s:taps - s, :]
    qk = acc * jax.nn.sigmoid(acc)
    q = qk[:, :B_WIDTH]
    k = qk[:, B_WIDTH:] * (B_HEAD_DIM ** -0.5)
    v = pb[:, 2 * B_WIDTH:3 * B_WIDTH]
    o = pb[:, 3 * B_WIDTH:]

    pif = pif_ref[...] + bifr_ref[...]
    lane = lax.broadcasted_iota(jnp.int32, pif.shape, 1)
    logf_c = jnp.where(lane >= B_HEADS, _log_sigmoid(pif), 0.0)
    f_c = _mm32(ltri_ref[...], logf_c)
    g = gt_ref[0] + bifc_ref[...]
    row = lax.broadcasted_iota(jnp.int32, g.shape, 0)
    logf_r = jnp.where(row >= B_HEADS, _log_sigmoid(g), 0.0)
    f_r = _mm32(logf_r, utri_ref[...])

    ti = lax.broadcasted_iota(jnp.int32, (L, L), 0)
    si = lax.broadcasted_iota(jnp.int32, (L, L), 1)
    causal = si <= ti

    W = B_WIDTH
    lane_head = lax.broadcasted_iota(jnp.int32, (L, W), 1) // B_HEAD_DIM
    lane_head_row = lax.broadcasted_iota(jnp.int32, (1, W), 1) // B_HEAD_DIM
    row_head_col = lax.broadcasted_iota(jnp.int32, (W, 1), 0) // B_HEAD_DIM
    same_head = (lax.broadcasted_iota(jnp.int32, (W, W), 0) // B_HEAD_DIM
                 == lax.broadcasted_iota(jnp.int32, (W, W), 1) // B_HEAD_DIM)
    c0 = c_ref[...]
    n0 = n_ref[...]
    q_c = _mm_nt(q, c0)
    q_n = q * n0
    inv_d = 1.0 / B_HEAD_DIM

    hh = jnp.zeros((L, W), F32)
    w_all = jnp.zeros((L, W), F32)
    cd_lane = jnp.zeros((1, W), F32)
    cd_rows = jnp.zeros((W, 1), F32)
    for h in range(B_HEADS):
        hm = lane_head == h
        fc = f_c[:, B_HEADS + h:B_HEADS + h + 1]
        igc = pif[:, h:h + 1]
        fr = f_r[B_HEADS + h:B_HEADS + h + 1, :]
        igr = g[h:h + 1, :]
        f_last = fr[:, L - 1:L]
        m0 = m_ref[h]

        d = jnp.where(causal, fc - fr + igr, -jnp.inf)
        g_inter = fc + m0
        m_t = jnp.maximum(g_inter, jnp.max(d, axis=-1, keepdims=True))
        sw = _mm_nt(jnp.where(hm, q, 0.0), k) * jnp.exp(d - m_t)
        inter = jnp.exp(g_inter - m_t)
        den = (jnp.sum(sw, axis=-1, keepdims=True)
               + inter * jnp.sum(jnp.where(hm, q_n, 0.0), axis=-1, keepdims=True))
        inv = 1.0 / jnp.maximum(jnp.abs(den), jnp.exp(-m_t))
        hh = jnp.where(hm, (_mm(sw, v) + inter * q_c) * inv, hh)

        g_end_r = f_last - fr + igr
        m_new = jnp.maximum(f_last + m0, jnp.max(g_end_r, axis=-1, keepdims=True))
        carry_decay = jnp.exp(f_last + m0 - m_new)
        w_all = jnp.where(hm, jnp.exp(f_last - fc + igc - m_new), w_all)
        cd_lane = jnp.where(lane_head_row == h, carry_decay, cd_lane)
        cd_rows = jnp.where(row_head_col == h, carry_decay, cd_rows)
        m_ref[h] = m_new

    sq = hh * hh
    ms = jnp.zeros((L, W), F32)
    for h in range(B_HEADS):
        hm = lane_head == h
        ms = jnp.where(hm, jnp.sum(jnp.where(hm, sq, 0.0), axis=-1, keepdims=True) * inv_d, ms)
    o_ref[...] = hh * lax.rsqrt(ms + EPS) * hng_ref[...] * jax.nn.sigmoid(o)

    c_ref[...] = cd_rows * c0 + jnp.where(same_head, _mm_tn(v * w_all, k), 0.0)
    n_ref[...] = cd_lane * n0 + jnp.sum(k * w_all, axis=0, keepdims=True)


def _mlstm(p_b, p_if, g_t, conv_w, conv_b, bif_row, bif_col, hn_g, ltri, utri, seq):
    nt = p_b.shape[0]
    L = B_CHUNK
    nc = seq // L
    return pl.pallas_call(
        functools.partial(_mlstm_kernel, chunks_per_seq=nc),
        grid=(nt // L,),
        in_specs=[pl.BlockSpec((L, 4 * B_WIDTH), lambda i: (i, 0)),
                  pl.BlockSpec((SUBLANES, 4 * B_WIDTH), lambda i: (jnp.maximum(i * (L // SUBLANES) - 1, 0), 0)),
                  pl.BlockSpec((L, LANES), lambda i: (i, 0)),
                  pl.BlockSpec((1, SUBLANES, L), lambda i: (i // nc, 0, i % nc)),
                  _full(conv_w.shape), _full(conv_b.shape), _full(bif_row.shape), _full(bif_col.shape),
                  _full(hn_g.shape), _full(ltri.shape), _full(utri.shape)],
        out_specs=pl.BlockSpec((L, B_WIDTH), lambda i: (i, 0)),
        out_shape=jax.ShapeDtypeStruct((nt, B_WIDTH), F32),
        scratch_shapes=[pltpu.VMEM((B_WIDTH, B_WIDTH), F32),
                        pltpu.VMEM((1, B_WIDTH), F32),
                        pltpu.VMEM((B_HEADS, 1, 1), F32)],
        compiler_params=_params(),
        name="mlstm",
    )(p_b, p_b, p_if, g_t, conv_w, conv_b, bif_row, bif_col, hn_g, ltri, utri)


def _swa_kernel(q_ref, kvc_ref, kvp_ref, qn_ref, kn_ref, sink_ref, o_ref, *, blocks_per_seq):
    i = pl.program_id(0)
    first = (i % blocks_per_seq) == 0
    bk = C_BLOCK
    hp = LANES
    group = SWA_STACK
    rows = group * bk
    kv = jnp.concatenate([kvp_ref[...], kvc_ref[...]], axis=0)
    qrow = lax.broadcasted_iota(jnp.int32, (rows, 2 * bk), 0)
    kj = lax.broadcasted_iota(jnp.int32, (rows, 2 * bk), 1)
    dist = qrow % bk + bk - kj
    valid = (dist >= 0) & (dist < WINDOW) & (kj >= jnp.where(first, bk, 0))
    distf = dist.astype(F32)
    rgroup = lax.broadcasted_iota(jnp.int32, (rows, 1), 0) // bk
    inv_d = 1.0 / C_HEAD_DIM
    qn = qn_ref[...]
    kn = kn_ref[...]

    def per_row_group(vals):
        out = vals[-1]
        for j in range(group - 2, -1, -1):
            out = jnp.where(rgroup == j, vals[j], out)
        return out

    ones_head = jnp.ones((hp, hp), F32)
    ones_keys = jnp.ones((2 * bk, hp), BF16)

    heads_per_kv = C_Q_HEADS // C_KV_HEADS
    kn_heads, v_heads = [], []
    for kvh in range(C_KV_HEADS):
        kh = kv[:, kvh * hp:(kvh + 1) * hp]
        kn_heads.append((kh * lax.rsqrt(_mm_f32_lhs(kh * kh, ones_head) * inv_d + EPS) * kn).astype(BF16))
        v_heads.append(kv[:, (C_KV_HEADS + kvh) * hp:(C_KV_HEADS + kvh + 1) * hp].astype(BF16))

    for first_head in range(0, C_Q_HEADS, group):
        heads = list(range(first_head, first_head + group))
        kh = kn_heads[first_head // heads_per_kv]
        vh = v_heads[first_head // heads_per_kv]
        q4 = jnp.concatenate([q_ref[:, h * hp:(h + 1) * hp] for h in heads], axis=0)
        q4 = q4 * lax.rsqrt(_mm_f32_lhs(q4 * q4, ones_head) * inv_d + EPS) * qn * (C_HEAD_DIM ** -0.5)
        slope = per_row_group([float(2.0 ** (-8.0 * (h + 1) / C_Q_HEADS)) for h in heads])
        sink = per_row_group([sink_ref[:, h:h + 1] for h in heads])
        s = jnp.where(valid, _mm_nt(q4, kh) - slope * distf, -jnp.inf)
        m = jnp.maximum(jnp.max(s, axis=-1, keepdims=True), sink)
        p = jnp.exp(s - m).astype(BF16)
        den = jnp.dot(p, ones_keys, preferred_element_type=F32) + jnp.exp(sink - m)
        out = jnp.dot(p, vh, preferred_element_type=F32) * (1.0 / den)
        for pair in range(group // 2):
            even = out[(2 * pair) * bk:(2 * pair + 1) * bk]
            odd = out[(2 * pair + 1) * bk:(2 * pair + 2) * bk]
            t = first_head // 2 + pair
            o_ref[:, t * hp:(t + 1) * hp] = even + pltpu.roll(odd, C_HEAD_DIM, 1)


def _swa(p_cq, p_ckv, qn_g, kn_g, sink_row, seq):
    nt = p_cq.shape[0]
    bk = C_BLOCK
    nb = seq // bk
    return pl.pallas_call(
        functools.partial(_swa_kernel, blocks_per_seq=nb),
        grid=(nt // bk,),
        in_specs=[pl.BlockSpec((bk, p_cq.shape[1]), lambda i: (i, 0)),
                  pl.BlockSpec((bk, p_ckv.shape[1]), lambda i: (i, 0)),
                  pl.BlockSpec((bk, p_ckv.shape[1]), lambda i: (jnp.maximum(i - 1, 0), 0)),
                  _full(qn_g.shape), _full(kn_g.shape), _full(sink_row.shape)],
        out_specs=pl.BlockSpec((bk, C_WIDTH), lambda i: (i, 0)),
        out_shape=jax.ShapeDtypeStruct((nt, C_WIDTH), F32),
        compiler_params=_params(),
        name="swa",
    )(p_cq, p_ckv, p_ckv, qn_g, kn_g, sink_row)


def _merge_kernel(x_ref, gt_ref, pg_ref, ya_ref, r_ref, k_ref, v_ref, g_ref, yb_ref, yc_ref,
                  hsum_ref, lng_ref, lnb_ref, rk_ref, wa_ref, wb_ref, wc_ref, wo_ref, o_ref):
    hsum = hsum_ref[...]
    inv_n = 1.0 / A_HEAD_DIM
    y = ya_ref[...]
    mu = _mm_f32_lhs(y, hsum) * inv_n
    dlt = y - mu
    var = _mm_f32_lhs(dlt * dlt, hsum) * inv_n
    yn = dlt * lax.rsqrt(var + A_LN_EPS) * lng_ref[...] + lnb_ref[...]
    bonus = _mm_f32_lhs(r_ref[...] * k_ref[...] * rk_ref[...], hsum) * v_ref[...]
    y_a = (yn + bonus) * g_ref[...]
    d = x_ref.shape[1]
    pg = pg_ref[...].astype(F32)
    merged = (jax.nn.sigmoid(pg[:, :d]) * _mm(y_a, wa_ref[...])
              + jax.nn.sigmoid(pg[:, d:2 * d]) * _mm(yb_ref[...], wb_ref[...])
              + jax.nn.sigmoid(pg[:, 2 * d:]) * _mm(yc_ref[...], wc_ref[...]))
    o_ref[...] = x_ref[...] + gt_ref[0] * _mm(merged, wo_ref[...])


def _merge(x2, gt, p_g, y_scan, r, k, v, g, y_b, y_c, hsum, lng, lnb, rk, wa, wb, wc, wo, seq):
    nt, d = x2.shape
    tm = ROW_TILE
    tps = seq // tm
    row = lambda w: pl.BlockSpec((tm, w), lambda i: (i, 0))
    consts = (hsum, lng, lnb, rk, wa, wb, wc, wo)
    return pl.pallas_call(
        _merge_kernel,
        grid=(nt // tm,),
        in_specs=[row(d), pl.BlockSpec((1, 1, d), lambda i: (i // tps, 0, 0)), row(3 * d)]
                 + [_time_major_spec(tm, A_WIDTH, tps)] * 5 + [row(B_WIDTH), row(C_WIDTH)]
                 + [_full(c.shape) for c in consts],
        out_specs=row(d),
        out_shape=jax.ShapeDtypeStruct((nt, d), F32),
        compiler_params=_params(),
        name="merge",
    )(x2, gt, p_g, y_scan, r, k, v, g, y_b, y_c, *consts)


def _route_kernel(x_ref, g_ref, sc_ref, sh_ref, wr_ref, br_ref, lstrict_ref, h_ref, idx_ref, wgt_ref, cnt_ref):
    x = x_ref[...]
    y = x * lax.rsqrt(jnp.mean(x * x, axis=-1, keepdims=True) + EPS) * g_ref[...]
    h = y * (1.0 + sc_ref[0]) + sh_ref[0]
    _store_token_tiles(h_ref, h)
    lg = _mm3(h, wr_ref[...]) + br_ref[...]
    lane = lax.broadcasted_iota(jnp.int32, lg.shape, 1)
    lanef = lane.astype(F32)
    ninf = -jnp.inf
    big = float(LANES)

    def first_argmax(vals, vmax):
        return jnp.min(jnp.where(vals == vmax, lanef, big), axis=-1, keepdims=True)

    gl = jnp.where(lane < N_GROUPS, lg, ninf)
    gmax = jnp.max(gl, axis=-1, keepdims=True)
    g_top = 1.0 / jnp.sum(jnp.exp(gl - gmax), axis=-1, keepdims=True)
    g_idx = first_argmax(gl, gmax)
    lo = N_GROUPS + EXPERTS_PER_GROUP * g_idx
    el = jnp.where((lanef >= lo) & (lanef < lo + EXPERTS_PER_GROUP), lg, ninf)
    e1 = jnp.max(el, axis=-1, keepdims=True)
    z = jnp.sum(jnp.exp(el - e1), axis=-1, keepdims=True)
    i1 = first_argmax(el, e1)
    el2 = jnp.where(lanef == i1, ninf, el)
    e2 = jnp.max(el2, axis=-1, keepdims=True)
    i2 = first_argmax(el2, e2)
    p1 = 1.0 / z
    p2 = jnp.exp(e2 - e1) / z
    w1 = g_top * p1 / (p1 + p2)
    w2 = g_top * p2 / (p1 + p2)
    oh1 = jnp.where(lanef == i1, 1.0, 0.0)
    oh2 = jnp.where(lanef == i2, 1.0, 0.0)
    both = oh1 + oh2
    before = _mm(lstrict_ref[...], both)
    r1 = jnp.sum(before * oh1, axis=-1, keepdims=True)
    r2 = jnp.sum(before * oh2, axis=-1, keepdims=True)
    idx = jnp.where(lane == 0, i1 - N_GROUPS, jnp.where(lane == 1, i2 - N_GROUPS,
                    jnp.where(lane == 2, r1, jnp.where(lane == 3, r2, 0.0))))
    idx_ref[...] = idx.astype(jnp.int32)
    wgt_ref[...] = jnp.where(lane == 0, w1, jnp.where(lane == 1, w2, 0.0))
    counts = jnp.sum(both, axis=0, keepdims=True)
    cnt_ref[0] = jnp.broadcast_to(counts, cnt_ref.shape[1:]).astype(jnp.int32)


def _route(x2, g, sc, sh, w_r, b_r, seq):
    nt, d = x2.shape
    tm = ROW_TILE
    tps = seq // tm
    n_tiles = nt // tm
    row = lambda w: pl.BlockSpec((tm, w), lambda i: (i, 0))
    mod = pl.BlockSpec((1, 1, d), lambda i: (i // tps, 0, 0))
    lstrict = jnp.tril(jnp.ones((tm, tm), F32), k=-1)
    return pl.pallas_call(
        _route_kernel,
        grid=(n_tiles,),
        in_specs=[row(d), _full((1, d)), mod, mod, _full(w_r.shape), _full(b_r.shape), _full(lstrict.shape)],
        out_specs=[pl.BlockSpec((tm * TILE_SLABS, LANES), lambda i: (i, 0)), row(LANES), row(LANES),
                   pl.BlockSpec((1, SUBLANES, LANES), lambda i: (i, 0, 0))],
        out_shape=[jax.ShapeDtypeStruct((nt * TILE_SLABS, LANES), F32), jax.ShapeDtypeStruct((nt, LANES), jnp.int32),
                   jax.ShapeDtypeStruct((nt, LANES), F32),
                   jax.ShapeDtypeStruct((n_tiles, SUBLANES, LANES), jnp.int32)],
        compiler_params=_params(),
        name="route",
    )(x2, g, sc, sh, w_r, b_r, lstrict)


def _expert_kernel(bexp_ref, tok_ref, tok_next_ref, h_hbm, win_ref, wout_ref, o_ref, buf0, buf1, sem,
                   win_bf, wout_bf):
    i = pl.program_id(0)

    @pl.when(jnp.logical_or(i == 0, bexp_ref[i] != bexp_ref[jnp.maximum(i - 1, 0)]))
    def _():
        win_bf[...] = win_ref[0, 0].astype(BF16)
        wout_bf[...] = wout_ref[0, 0].astype(BF16)

    def body(rows):
        gu = None
        for s in range(0, TILE_SLABS, 2):
            xs = jnp.concatenate([_slab(rows, 0, MOE_BLOCK, s), _slab(rows, 0, MOE_BLOCK, s + 1)], axis=1)
            part = _mm(xs, win_bf[s * LANES:(s + 2) * LANES, :])
            gu = part if gu is None else gu + part
        gate = gu[:, :D_EXPERT]
        up = gu[:, D_EXPERT:]
        _store_token_tiles(o_ref, _mm(gate * jax.nn.sigmoid(gate) * up, wout_bf[...]))

    _with_gathered_rows(tok_ref, tok_next_ref, MOE_BLOCK, h_hbm, (buf0, buf1), sem, body)


def _experts(block_exp, slot_row, h2, w_in_e, w_out_e, layer):
    n_blocks = block_exp.shape[0]
    rows = MOE_BLOCK
    d = TILE_SLABS * LANES
    tok3 = slot_row.reshape(n_blocks, 1, rows)
    smem_blk = lambda f: pl.BlockSpec((1, 1, rows), f, memory_space=pltpu.SMEM)
    grid_spec = pltpu.PrefetchScalarGridSpec(
        num_scalar_prefetch=1,
        grid=(n_blocks,),
        in_specs=[smem_blk(lambda i, be: (i, 0, 0)),
                  smem_blk(lambda i, be: (jnp.minimum(i + 1, n_blocks - 1), 0, 0)),
                  pl.BlockSpec(memory_space=pl.ANY),
                  pl.BlockSpec((1, 1, d, 2 * D_EXPERT), lambda i, be: (layer, be[i], 0, 0)),
                  pl.BlockSpec((1, 1, D_EXPERT, d), lambda i, be: (layer, be[i], 0, 0))],
        out_specs=pl.BlockSpec((rows * TILE_SLABS, LANES), lambda i, be: (i, 0)),
        scratch_shapes=[pltpu.VMEM((rows * TILE_SLABS, LANES), F32)] * 2 + [pltpu.SemaphoreType.DMA((2,)),
                        pltpu.VMEM((d, 2 * D_EXPERT), BF16), pltpu.VMEM((D_EXPERT, d), BF16)],
    )
    return pl.pallas_call(
        _expert_kernel,
        grid_spec=grid_spec,
        out_shape=jax.ShapeDtypeStruct((n_blocks * rows * TILE_SLABS, LANES), F32),
        compiler_params=_params(),
        name="experts",
    )(block_exp, tok3, tok3, h2, w_in_e, w_out_e)


def _combine_kernel(pos_ref, pos_next_ref, x_ref, gt_ref, wgt_ref, yb_hbm, o_ref, buf0, buf1, sem):
    def emit(x):
        o_ref[...] = x

    _moe_residual(pos_ref, pos_next_ref, x_ref, gt_ref, wgt_ref, yb_hbm, (buf0, buf1), sem, emit)


def _combine(pos, x2, gt, wgt, yb, seq):
    nt, d = x2.shape
    tm = ROW_TILE
    n_tiles = nt // tm
    pos3, specs, scratch = _moe_residual_specs(pos, tm, n_tiles, d, seq // tm)
    return pl.pallas_call(
        _combine_kernel,
        grid=(n_tiles,),
        in_specs=specs,
        out_specs=pl.BlockSpec((tm, d), lambda i: (i, 0)),
        out_shape=jax.ShapeDtypeStruct((nt, d), F32),
        scratch_shapes=scratch,
        compiler_params=_params(),
        name="combine",
    )(pos3, pos3, x2, gt, wgt, yb)


def _slot_rows_kernel(dest_ref, out_ref):
    def clear(s, carry):
        out_ref[s] = 0
        return carry

    def put(a, carry):
        token = lax.shift_right_logical(a, TOP_K.bit_length() - 1)
        out_ref[dest_ref[a]] = token * TILE_SLABS
        return carry

    lax.fori_loop(0, out_ref.shape[0], clear, 0, unroll=8)
    lax.fori_loop(0, dest_ref.shape[0], put, 0, unroll=8)


def _slot_rows(dest_flat, n_slots):
    assert TOP_K & (TOP_K - 1) == 0
    smem = pl.BlockSpec(memory_space=pltpu.SMEM)
    return pl.pallas_call(
        _slot_rows_kernel,
        in_specs=[smem],
        out_specs=smem,
        out_shape=jax.ShapeDtypeStruct((n_slots,), jnp.int32),
        name="slot_rows",
    )(dest_flat)


def _dispatch_plan(ridx, tile_counts, n_tokens):
    n_tiles = tile_counts.shape[0]
    tm = n_tokens // n_tiles
    na = n_tokens * TOP_K
    counts = tile_counts[:, 0, N_GROUPS:N_GROUPS + N_EXPERTS]
    tile_off = jnp.cumsum(counts, axis=0) - counts
    total = jnp.sum(counts, axis=0)
    padded = (total + MOE_BLOCK - 1) // MOE_BLOCK * MOE_BLOCK
    pad_ends = jnp.cumsum(padded)
    base = (pad_ends - padded)[None, :] + tile_off
    expert = ridx[:, :TOP_K].reshape(n_tiles, tm, TOP_K)
    rank = ridx[:, TOP_K:2 * TOP_K].reshape(n_tiles, tm, TOP_K)
    onehot = expert[..., None] == jnp.arange(N_EXPERTS, dtype=jnp.int32)
    dest = jnp.sum(jnp.where(onehot, base[:, None, None, :], 0), axis=-1) + rank
    dest = dest.reshape(n_tokens, TOP_K)
    n_blocks = -(-na // MOE_BLOCK) + N_EXPERTS
    slot_row = _slot_rows(dest.reshape(-1), n_blocks * MOE_BLOCK)
    block_start = jnp.arange(n_blocks, dtype=jnp.int32) * MOE_BLOCK
    block_exp = jnp.minimum(jnp.sum((pad_ends[None, :] <= block_start[:, None]).astype(jnp.int32), axis=1),
                            N_EXPERTS - 1)
    return block_exp, slot_row, dest * TILE_SLABS


def _pad_cols(w, width):
    return jnp.pad(w, ((0, 0), (0, width - w.shape[1])))


def _pad_heads(w, head_dim):
    rows = w.shape[0]
    w = w.reshape(rows, -1, head_dim)
    return jnp.pad(w, ((0, 0), (0, 0), (0, LANES - head_dim))).reshape(rows, -1)


def kernel(x, c, w_ada, b_ada, norm1_g, norm2_g, w_in, mu_a, w0_a, w2_a, a0_a, a2_a, g2_a, kk_a, ka_a, rk_a,
           lnx_g_a, lnx_b_a, conv_w_b, conv_b_b, bi_b, bf_b, hn_g_b, qn_g_c, kn_g_c, sink_c, w_br_a, w_br_b,
           w_br_c, w_out, w_rg, b_rg, w_re, b_re, w_e_in, w_e_out):
    bsz, seq, d = x.shape
    nt = bsz * seq
    depth = w_in.shape[0]
    assert bsz * A_HEADS * VAL_LANE_REPS == LANES, "the RWKV recurrence packs (batch, head, 4 value indices) on lanes"
    assert seq % ROW_TILE == 0 and seq % B_CHUNK == 0 and seq % SCAN_STEPS == 0

    mod = _ada_mod(c, w_ada, b_ada)
    hsum = jnp.kron(jnp.eye(A_HEADS, dtype=F32), jnp.ones((A_HEAD_DIM, A_HEAD_DIM), F32))
    tri = jnp.tril(jnp.ones((B_CHUNK, B_CHUNK), F32))

    o_b = 1024
    o_if = o_b + 4 * B_WIDTH
    o_cq = o_if + 2 * B_HEADS
    o_ckv = o_cq + C_WIDTH
    o_g = o_ckv + 2 * C_KV_WIDTH

    x2 = x.reshape(nt, d)
    moe = None
    for l in range(depth):
        m = mod[l].reshape(bsz, 6, 1, d)
        sh1, sc1, gt1, sh2, sc2, gt2 = (m[:, i] for i in range(6))
        wl = w_in[l]
        weights = [wl[:, o_g:].astype(BF16), wl[:, :o_b].astype(BF16), wl[:, o_b:o_if].astype(BF16),
                   _pad_heads(wl[:, o_cq:o_ckv], C_HEAD_DIM).astype(BF16),
                   _pad_heads(wl[:, o_ckv:o_g], C_HEAD_DIM).astype(BF16),
                   _pad_cols(wl[:, o_if:o_cq], LANES).astype(BF16)]
        outs = _norm_proj(x2, norm1_g[l][None], sc1, sh1, weights, [BF16] + [F32] * 5, seq, moe)
        if moe is not None:
            x2 = outs.pop(0)
        p_g, p_a, p_b, p_cq, p_ckv, p_if = outs

        wwa = (jnp.zeros((A_DECAY_LORA + A_ICL_LORA, 2 * A_WIDTH), F32)
               .at[:A_DECAY_LORA, :A_WIDTH].set(w2_a[l]).at[A_DECAY_LORA:, A_WIDTH:].set(a2_a[l]))
        r, dec, k2, v, kk, bvec, g = _rwkv_prep(p_a, mu_a[l][None], wwa, w0_a[l][None], a0_a[l][None], g2_a[l],
                                                kk_a[l][None], ka_a[l][None], hsum, seq)
        y_scan = _rwkv_scan(*(_key_layout(t, bsz, seq) for t in (dec, kk, bvec, k2, r)),
                            _value_layout(v, bsz, seq))
        y_scan = _value_unlayout(y_scan, bsz, seq)

        g_t = p_if[:, :SUBLANES].reshape(bsz, seq, SUBLANES).transpose(0, 2, 1)
        bif = jnp.concatenate([bi_b[l], bf_b[l]])
        y_b = _mlstm(p_b, p_if, g_t, conv_w_b[l], conv_b_b[l][None], _pad_cols(bif[None], LANES), bif[:, None],
                     hn_g_b[l].reshape(1, B_WIDTH), tri, tri.T, seq)

        y_c = _swa(p_cq, p_ckv, _pad_cols(qn_g_c[l][None], LANES), _pad_cols(kn_g_c[l][None], LANES),
                   _pad_cols(sink_c[l][None], LANES), seq)

        x2 = _merge(x2, gt1, p_g, y_scan, r, k2, v, g, y_b, y_c, hsum, lnx_g_a[l][None], lnx_b_a[l][None],
                    rk_a[l].reshape(1, A_WIDTH), w_br_a[l].astype(BF16), w_br_b[l].astype(BF16),
                    w_br_c[l].astype(BF16), w_out[l].astype(BF16), seq)

        w_r = _pad_cols(jnp.concatenate([w_rg[l], w_re[l]], axis=1), LANES)
        b_r = _pad_cols(jnp.concatenate([b_rg[l], b_re[l]])[None], LANES)
        h2, ridx, rwgt, tile_counts = _route(x2, norm2_g[l][None], sc2, sh2, w_r, b_r, seq)
        block_exp, slot_row, pos = _dispatch_plan(ridx, tile_counts, nt)
        yb = _experts(block_exp, slot_row, h2, w_e_in, w_e_out, l)
        moe = (pos, gt2, rwgt, yb)
    pos, gt2, rwgt, yb = moe
    x2 = _combine(pos, x2, gt2, rwgt, yb, seq)
    return x2.reshape(bsz, seq, d)
```
